```python
import jax, jax.numpy as jnp
from jax import lax
import numpy as np

D_MODEL = 1024
BATCH = 8
SEQ = 2048
DEPTH = 4

N_HEADS = 8
HEAD_DIM = 64
ATTN_WIDTH = N_HEADS * HEAD_DIM
CONV_CH = 512
CONV_K = 31
D_FF = 2816
D_PLE = 256
Q_BLOCK = 128
EPS = 1e-6
FFN_RES = 0.5

Q0 = 0
K0 = Q0 + ATTN_WIDTH
V0 = K0 + ATTN_WIDTH
F0 = V0 + ATTN_WIDTH
C0 = F0 + N_HEADS
GA0 = C0 + 2 * CONV_CH
GC0 = GA0 + D_MODEL
IN_COLS = GC0 + D_MODEL

kernel_name = "macaron_fox_conformer_hybrid"


def rmsnorm(x, g):
    xf = x.astype(jnp.float32)
    y = xf * lax.rsqrt(jnp.mean(xf * xf, axis=-1, keepdims=True) + EPS)
    return (y * g.astype(jnp.float32)).astype(x.dtype)


def swiglu(x, w_in, w_out):
    a, b = jnp.split(x @ w_in, 2, axis=-1)
    return (jax.nn.silu(a) * b) @ w_out


def forgetting_attention(q, k, v, f_logit):
    b, s, h, dh = q.shape
    scale = 1.0 / float(np.sqrt(dh))
    c = jnp.cumsum(jax.nn.log_sigmoid(f_logit.astype(jnp.float32)), axis=1)
    c = jnp.transpose(c, (0, 2, 1))
    qh = jnp.transpose(q, (0, 2, 1, 3))
    kh = jnp.transpose(k, (0, 2, 1, 3))
    vh = jnp.transpose(v, (0, 2, 1, 3))
    outs = []
    for blk in range(s // Q_BLOCK):
        qs, qe = blk * Q_BLOCK, (blk + 1) * Q_BLOCK
        sc = jnp.einsum('bhqd,bhkd->bhqk', qh[:, :, qs:qe], kh[:, :, :qe]).astype(jnp.float32) * scale
        sc = sc + (c[:, :, qs:qe, None] - c[:, :, None, :qe])
        causal = jnp.arange(qs, qe)[:, None] >= jnp.arange(qe)[None, :]
        sc = jnp.where(causal[None, None], sc, -jnp.inf)
        pr = jax.nn.softmax(sc, axis=-1).astype(vh.dtype)
        outs.append(jnp.einsum('bhqk,bhkd->bhqd', pr, vh[:, :, :qe]))
    o = jnp.concatenate(outs, axis=2)
    return jnp.transpose(o, (0, 2, 1, 3)).reshape(b, s, h * dh)


def conformer_conv(glu_in, conv_w, conv_b, g_conv):
    a = glu_in[..., :CONV_CH] * jax.nn.sigmoid(glu_in[..., CONV_CH:])
    y = lax.conv_general_dilated(
        a, conv_w[:, None, :].astype(a.dtype), window_strides=(1,),
        padding=[(CONV_K - 1, 0)], dimension_numbers=('NWC', 'WIO', 'NWC'),
        feature_group_count=CONV_CH) + conv_b
    return jax.nn.silu(rmsnorm(y, g_conv))


def hybrid_mixer(u, w_in, b_f, w_attn_out, conv_w, conv_b, g_conv, w_conv_out, w_out):
    b, s, _ = u.shape
    z = u @ w_in
    q = z[..., Q0:K0].reshape(b, s, N_HEADS, HEAD_DIM)
    k = z[..., K0:V0].reshape(b, s, N_HEADS, HEAD_DIM)
    v = z[..., V0:F0].reshape(b, s, N_HEADS, HEAD_DIM)
    f_logit = z[..., F0:C0] + b_f
    y_attn = forgetting_attention(q, k, v, f_logit) @ w_attn_out
    y_conv = conformer_conv(z[..., C0:GA0], conv_w, conv_b, g_conv) @ w_conv_out
    merged = jax.nn.sigmoid(z[..., GA0:GC0]) * y_attn + jax.nn.sigmoid(z[..., GC0:]) * y_conv
    return merged @ w_out


def setup_inputs(seed: int = 0) -> dict:
    key = jax.random.key(seed)
    ks = jax.random.split(key, 24)
    L, D, F = DEPTH, D_MODEL, D_FF
    f32 = jnp.float32

    def w(k, shape, fan_in):
        return jax.random.normal(k, shape, f32) * (fan_in ** -0.5)

    def gain(k, shape):
        return 1.0 + 0.05 * jax.random.normal(k, shape, f32)

    return {
        "x": jax.random.normal(ks[0], (BATCH, SEQ, D), f32),
        "p": jax.random.normal(ks[1], (DEPTH, BATCH, SEQ, D_PLE), f32),
        "g_ff1": gain(ks[2], (L, D)),
        "w_ff1_in": w(ks[3], (L, D, 2 * F), D),
        "w_ff1_out": w(ks[4], (L, F, D), F),
        "g_mix": gain(ks[5], (L, D)),
        "w_in": w(ks[6], (L, D, IN_COLS), D),
        "b_f": 2.0 + 0.5 * jax.random.normal(ks[7], (L, N_HEADS), f32),
        "w_attn_out": w(ks[8], (L, ATTN_WIDTH, D), ATTN_WIDTH),
        "conv_w": w(ks[9], (L, CONV_K, CONV_CH), CONV_K),
        "conv_b": 0.02 * jax.random.normal(ks[10], (L, CONV_CH), f32),
        "g_conv": gain(ks[11], (L, CONV_CH)),
        "w_conv_out": w(ks[12], (L, CONV_CH, D), CONV_CH),
        "w_out": w(ks[13], (L, D, D), D),
        "g_ff2": gain(ks[14], (L, D)),
        "w_ff2_in": w(ks[15], (L, D, 2 * F), D),
        "w_ff2_out": w(ks[16], (L, F, D), F),
        "g_ple": gain(ks[17], (L, D)),
        "w_ple_gate": w(ks[18], (L, D, D), D),
        "w_ple_proj": w(ks[19], (L, D_PLE, D), D_PLE),
        "g_final": gain(ks[20], (D,)),
    }


def reference(x, p, g_ff1, w_ff1_in, w_ff1_out, g_mix, w_in, b_f, w_attn_out,
              conv_w, conv_b, g_conv, w_conv_out, w_out, g_ff2, w_ff2_in, w_ff2_out,
              g_ple, w_ple_gate, w_ple_proj, g_final):
    h = x
    for i in range(DEPTH):
        h = h + FFN_RES * swiglu(rmsnorm(h, g_ff1[i]), w_ff1_in[i], w_ff1_out[i])
        h = h + hybrid_mixer(rmsnorm(h, g_mix[i]), w_in[i], b_f[i], w_attn_out[i],
                             conv_w[i], conv_b[i], g_conv[i], w_conv_out[i], w_out[i])
        h = h + FFN_RES * swiglu(rmsnorm(h, g_ff2[i]), w_ff2_in[i], w_ff2_out[i])
        gate = jax.nn.sigmoid(rmsnorm(h, g_ple[i]) @ w_ple_gate[i])
        h = h + gate * (p[i] @ w_ple_proj[i])
    return rmsnorm(h, g_final)
```

```python
import functools

import jax
import jax.numpy as jnp
from jax import lax
from jax.experimental import pallas as pl
from jax.experimental.pallas import tpu as pltpu

D_MODEL = 1024
BATCH = 8
SEQ = 2048
DEPTH = 4
N_HEADS = 8
HEAD_DIM = 64
ATTN_WIDTH = N_HEADS * HEAD_DIM
CONV_CH = 512
CONV_K = 31
D_FF = 2816
D_PLE = 256
EPS = 1e-6
FFN_RES = 0.5

F0 = 3 * ATTN_WIDTH
C0 = F0 + N_HEADS
GA0 = C0 + 2 * CONV_CH

LANES = 128
HEAD_ROWS = 16
HALO = 32
TM = 512
FFN_CHUNK = 256
MIX_CHUNK = 256
TQ = 256
TK = 256
CONV_ROWS = 64
VMEM_LIMIT = 56 * 1024 * 1024

BF16 = jnp.bfloat16
F32 = jnp.float32


def _dot(a, b):
    return jnp.dot(a, b, preferred_element_type=F32)


def _dot_nt(a, b):
    return lax.dot_general(a, b, (((1,), (1,)), ((), ())), preferred_element_type=F32)


def _rms(x, g):
    ms = jnp.mean(x * x, axis=-1, keepdims=True)
    return x * lax.rsqrt(ms + EPS) * g


def _sigmoid(x):
    return 1.0 / (1.0 + jnp.exp(-x))


def _resident(shape):
    nd = len(shape)
    return pl.BlockSpec(shape, lambda *_: (0,) * nd, pipeline_mode=pl.Buffered(1))


def _layer_block(shape, layer):
    nd = len(shape)
    return pl.BlockSpec((None,) + tuple(shape), lambda *_: (layer,) + (0,) * nd,
                        pipeline_mode=pl.Buffered(1))


def _ffn_body(*refs, with_ple, with_final):
    h_ref, g_ref, win_ref, wout_ref = refs[:4]
    pos = 4
    if with_ple:
        p_ref, gple_ref, wpg_ref, wpp_ref = refs[pos:pos + 4]
        pos += 4
    if with_final:
        gfin_ref = refs[pos]
        pos += 1
    o_ref, u_scr, act_scr = refs[pos:pos + 3]

    x = h_ref[...]
    u_scr[...] = _rms(x, g_ref[...]).astype(BF16)
    for c in range(D_FF // FFN_CHUNK):
        lo = c * FFN_CHUNK
        u = u_scr[...]
        a = _dot(u, win_ref[:, lo:lo + FFN_CHUNK])
        b = _dot(u, win_ref[:, D_FF + lo:D_FF + lo + FFN_CHUNK])
        act_scr[:, lo:lo + FFN_CHUNK] = (a * _sigmoid(a) * b).astype(BF16)
    h1 = x + FFN_RES * _dot(act_scr[...], wout_ref[...])
    if with_ple:
        u2 = _rms(h1, gple_ref[...]).astype(BF16)
        gate = _sigmoid(_dot(u2, wpg_ref[...]))
        h1 = h1 + gate * _dot(p_ref[...].astype(BF16), wpp_ref[...])
    if with_final:
        h1 = _rms(h1, gfin_ref[...])
    o_ref[...] = h1


def _ffn_call(h, g, w_in, w_out, layer, ple=None, g_final=None):
    m = h.shape[0]
    row = lambda i: (i, 0)
    in_specs = [
        pl.BlockSpec((TM, D_MODEL), row),
        _layer_block((1, D_MODEL), layer),
        _layer_block((D_MODEL, 2 * D_FF), layer),
        _layer_block((D_FF, D_MODEL), layer),
    ]
    args = [h, g, w_in, w_out]
    if ple is not None:
        p, g_ple, w_pg, w_pp = ple
        in_specs += [
            pl.BlockSpec((None, TM, D_PLE), lambda i: (layer, i, 0)),
            _layer_block((1, D_MODEL), layer),
            _layer_block((D_MODEL, D_MODEL), layer),
            _layer_block((D_PLE, D_MODEL), layer),
        ]
        args += [p, g_ple, w_pg, w_pp]
    if g_final is not None:
        in_specs.append(_resident((1, D_MODEL)))
        args.append(g_final)
    body = functools.partial(_ffn_body, with_ple=ple is not None, with_final=g_final is not None)
    return pl.pallas_call(
        body,
        grid=(m // TM,),
        in_specs=in_specs,
        out_specs=pl.BlockSpec((TM, D_MODEL), row),
        out_shape=jax.ShapeDtypeStruct((m, D_MODEL), F32),
        scratch_shapes=[pltpu.VMEM((TM, D_MODEL), BF16), pltpu.VMEM((TM, D_FF), BF16)],
        compiler_params=pltpu.CompilerParams(dimension_semantics=("arbitrary",),
                                             vmem_limit_bytes=VMEM_LIMIT),
        name="ffn_ple" if ple is not None else "ffn",
    )(*args)


def _proj_body(h_ref, g_ref, wqkv_ref, wc_ref, wft_ref, bf_ref,
               q_ref, k_ref, v_ref, a_ref, ct_ref, carry_scr):
    @pl.when(pl.program_id(1) == 0)
    def _():
        carry_scr[...] = jnp.zeros_like(carry_scr)

    u = _rms(h_ref[...], g_ref[...]).astype(BF16)
    scale = 1.0 / (HEAD_DIM ** 0.5)
    q_ref[...] = (_dot(u, wqkv_ref[:, 0:ATTN_WIDTH]) * scale).astype(BF16)
    k_ref[...] = _dot(u, wqkv_ref[:, ATTN_WIDTH:2 * ATTN_WIDTH]).astype(BF16)
    v_ref[...] = _dot(u, wqkv_ref[:, 2 * ATTN_WIDTH:3 * ATTN_WIDTH]).astype(BF16)
    glu_a = _dot(u, wc_ref[:, 0:CONV_CH])
    glu_b = _dot(u, wc_ref[:, CONV_CH:2 * CONV_CH])
    a_ref[...] = glu_a * _sigmoid(glu_b)

    f_t = _dot_nt(wft_ref[...], u) + bf_ref[...]
    ls = jnp.minimum(f_t, 0.0) - jnp.log1p(jnp.exp(-jnp.abs(f_t)))
    hi = ls.astype(BF16)
    r1 = ls - hi.astype(F32)
    mid = r1.astype(BF16)
    lo = (r1 - mid.astype(F32)).astype(BF16)
    pieces = jnp.concatenate([hi, mid, lo], axis=0)
    src = lax.broadcasted_iota(jnp.int32, (TM, TM), 0)
    dst = lax.broadcasted_iota(jnp.int32, (TM, TM), 1)
    tri = jnp.where(src <= dst, 1.0, 0.0).astype(BF16)
    cs3 = _dot(pieces, tri)
    cs = (cs3[0:HEAD_ROWS] + cs3[HEAD_ROWS:2 * HEAD_ROWS] + cs3[2 * HEAD_ROWS:3 * HEAD_ROWS]
          + carry_scr[:, 0:1])
    ct_ref[...] = cs[0:N_HEADS]
    carry_scr[...] = jnp.broadcast_to(cs[:, TM - 1:TM], carry_scr.shape)


def _proj_call(h3, g, w_qkv, w_c, w_ft, b_f, layer):
    n_s = SEQ // TM
    tile = lambda b, s: (b, s, 0)
    return pl.pallas_call(
        _proj_body,
        grid=(BATCH, n_s),
        in_specs=[
            pl.BlockSpec((None, TM, D_MODEL), tile),
            _layer_block((1, D_MODEL), layer),
            _layer_block((D_MODEL, 3 * ATTN_WIDTH), layer),
            _layer_block((D_MODEL, 2 * CONV_CH), layer),
            _layer_block((HEAD_ROWS, D_MODEL), layer),
            _layer_block((HEAD_ROWS, 1), layer),
        ],
        out_specs=[
            pl.BlockSpec((None, TM, ATTN_WIDTH), tile),
            pl.BlockSpec((None, TM, ATTN_WIDTH), tile),
            pl.BlockSpec((None, TM, ATTN_WIDTH), tile),
            pl.BlockSpec((None, TM, CONV_CH), tile),
            pl.BlockSpec((None, N_HEADS, TM), lambda b, s: (b, 0, s)),
        ],
        out_shape=[
            jax.ShapeDtypeStruct((BATCH, SEQ, ATTN_WIDTH), BF16),
            jax.ShapeDtypeStruct((BATCH, SEQ, ATTN_WIDTH), BF16),
            jax.ShapeDtypeStruct((BATCH, SEQ, ATTN_WIDTH), BF16),
            jax.ShapeDtypeStruct((BATCH, SEQ, CONV_CH), F32),
            jax.ShapeDtypeStruct((BATCH, N_HEADS, SEQ), F32),
        ],
        scratch_shapes=[pltpu.VMEM((HEAD_ROWS, LANES), F32)],
        compiler_params=pltpu.CompilerParams(dimension_semantics=("arbitrary", "arbitrary"),
                                             vmem_limit_bytes=VMEM_LIMIT),
        name="proj",
    )(h3, g, w_qkv, w_c, w_ft, b_f)


def _attn_body(q_ref, k_ref, v_ref, c_ref, o_ref, m_scr, l_scr, acc_scr):
    i = pl.program_id(1)
    half = HEAD_DIM
    lane = lax.broadcasted_iota(jnp.int32, (TQ, LANES), 1)
    row = lax.broadcasted_iota(jnp.int32, (TQ, TK), 0)
    col = lax.broadcasted_iota(jnp.int32, (TQ, TK), 1)
    causal = col <= row

    for g in range(N_HEADS // 2):
        cols = slice(g * LANES, (g + 1) * LANES)
        q_blk = q_ref[:, cols]
        zero = jnp.zeros_like(q_blk)
        q2 = jnp.concatenate([jnp.where(lane < half, q_blk, zero),
                              jnp.where(lane >= half, q_blk, zero)], axis=0)

        def scores(j, masked):
            k_blk = k_ref[pl.ds(j * TK, TK), cols]
            s = _dot_nt(q2, k_blk)
            t_e = s[0:TQ] - c_ref[2 * g:2 * g + 1, pl.ds(j * TK, TK)]
            t_o = s[TQ:2 * TQ] - c_ref[2 * g + 1:2 * g + 2, pl.ds(j * TK, TK)]
            if masked:
                t_e = jnp.where(causal, t_e, -jnp.inf)
                t_o = jnp.where(causal, t_o, -jnp.inf)
            return jnp.concatenate([t_e, t_o], axis=0)

        jd = pl.multiple_of(i * (TQ // TK), 1)
        t = scores(jd, True)
        m0 = jnp.max(t, axis=1, keepdims=True)
        p = jnp.exp(t - m0)
        m_scr[...] = m0
        l_scr[...] = jnp.sum(p, axis=1, keepdims=True)
        acc_scr[...] = _dot(p.astype(BF16), v_ref[pl.ds(jd * TK, TK), cols])

        def body(j, carry):
            t = scores(j, False)
            m_old = m_scr[...]
            m_new = jnp.maximum(m_old, jnp.max(t, axis=1, keepdims=True))
            alpha = jnp.exp(m_old - m_new)
            p = jnp.exp(t - m_new)
            m_scr[...] = m_new
            l_scr[...] = alpha * l_scr[...] + jnp.sum(p, axis=1, keepdims=True)
            acc_scr[...] = alpha * acc_scr[...] + _dot(p.astype(BF16), v_ref[pl.ds(j * TK, TK), cols])
            return carry

        lax.fori_loop(0, jd, body, 0)

        out = acc_scr[...] / l_scr[...]
        o_ref[:, cols] = jnp.where(lane < half, out[0:TQ], out[TQ:2 * TQ]).astype(BF16)


def _attn_call(q, k, v, c_t):
    tile = lambda b, i: (b, i, 0)
    whole = lambda b, i: (b, 0, 0)
    return pl.pallas_call(
        _attn_body,
        grid=(BATCH, SEQ // TQ),
        in_specs=[
            pl.BlockSpec((None, TQ, ATTN_WIDTH), tile),
            pl.BlockSpec((None, SEQ, ATTN_WIDTH), whole),
            pl.BlockSpec((None, SEQ, ATTN_WIDTH), whole),
            pl.BlockSpec((None, N_HEADS, SEQ), whole),
        ],
        out_specs=pl.BlockSpec((None, TQ, ATTN_WIDTH), tile),
        out_shape=jax.ShapeDtypeStruct((BATCH, SEQ, ATTN_WIDTH), BF16),
        scratch_shapes=[pltpu.VMEM((2 * TQ, 1), F32), pltpu.VMEM((2 * TQ, 1), F32),
                        pltpu.VMEM((2 * TQ, LANES), F32)],
        compiler_params=pltpu.CompilerParams(dimension_semantics=("arbitrary", "arbitrary"),
                                             vmem_limit_bytes=VMEM_LIMIT),
        name="attn",
    )(q, k, v, c_t)


def _conv_body(a_ref, halo_ref, w_ref, b_ref, g_ref, o_ref, xp_scr):
    first = pl.program_id(1) == 0
    halo = halo_ref[...]
    xp_scr[0:HALO, :] = jnp.where(first, jnp.zeros_like(halo), halo)
    xp_scr[HALO:HALO + TM, :] = a_ref[...]
    shift = HALO - (CONV_K - 1)
    for r in range(TM // CONV_ROWS):
        r0 = r * CONV_ROWS
        acc = jnp.broadcast_to(b_ref[...], (CONV_ROWS, CONV_CH))
        for j in range(CONV_K):
            acc = acc + w_ref[j:j + 1, :] * xp_scr[r0 + shift + j:r0 + shift + j + CONV_ROWS, :]
        y = _rms(acc, g_ref[...])
        o_ref[r0:r0 + CONV_ROWS, :] = (y * _sigmoid(y)).astype(BF16)


def _conv_call(a, conv_w, conv_b, g_conv, layer):
    n_h = TM // HALO
    return pl.pallas_call(
        _conv_body,
        grid=(BATCH, SEQ // TM),
        in_specs=[
            pl.BlockSpec((None, TM, CONV_CH), lambda b, s: (b, s, 0)),
            pl.BlockSpec((None, HALO, CONV_CH), lambda b, s: (b, jnp.maximum(s * n_h - 1, 0), 0)),
            _layer_block((HALO, CONV_CH), layer),
            _layer_block((1, CONV_CH), layer),
            _layer_block((1, CONV_CH), layer),
        ],
        out_specs=pl.BlockSpec((None, TM, CONV_CH), lambda b, s: (b, s, 0)),
        out_shape=jax.ShapeDtypeStruct((BATCH, SEQ, CONV_CH), BF16),
        scratch_shapes=[pltpu.VMEM((HALO + TM, CONV_CH), F32)],
        compiler_params=pltpu.CompilerParams(dimension_semantics=("arbitrary", "arbitrary"),
                                             vmem_limit_bytes=VMEM_LIMIT),
        name="conv",
    )(a, a, conv_w, conv_b, g_conv)


def _mix_body(h_ref, g_ref, oa_ref, yc_ref, wg_ref, wao_ref, wco_ref, wout_ref, o_ref, m_scr):
    x = h_ref[...]
    u = _rms(x, g_ref[...]).astype(BF16)
    oa = oa_ref[...]
    yc = yc_ref[...]
    for c in range(D_MODEL // MIX_CHUNK):
        lo = c * MIX_CHUNK
        ga = _sigmoid(_dot(u, wg_ref[:, lo:lo + MIX_CHUNK]))
        gc = _sigmoid(_dot(u, wg_ref[:, D_MODEL + lo:D_MODEL + lo + MIX_CHUNK]))
        ya = _dot(oa, wao_ref[:, lo:lo + MIX_CHUNK])
        yv = _dot(yc, wco_ref[:, lo:lo + MIX_CHUNK])
        m_scr[:, lo:lo + MIX_CHUNK] = (ga * ya + gc * yv).astype(BF16)
    o_ref[...] = x + _dot(m_scr[...], wout_ref[...])


def _mix_call(h, g, o_attn, y_conv, w_g, w_ao, w_co, w_out, layer):
    m = h.shape[0]
    row = lambda i: (i, 0)
    return pl.pallas_call(
        _mix_body,
        grid=(m // TM,),
        in_specs=[
            pl.BlockSpec((TM, D_MODEL), row),
            _layer_block((1, D_MODEL), layer),
            pl.BlockSpec((TM, ATTN_WIDTH), row),
            pl.BlockSpec((TM, CONV_CH), row),
            _layer_block((D_MODEL, 2 * D_MODEL), layer),
            _layer_block((ATTN_WIDTH, D_MODEL), layer),
            _layer_block((CONV_CH, D_MODEL), layer),
            _layer_block((D_MODEL, D_MODEL), layer),
        ],
        out_specs=pl.BlockSpec((TM, D_MODEL), row),
        out_shape=jax.ShapeDtypeStruct((m, D_MODEL), F32),
        scratch_shapes=[pltpu.VMEM((TM, D_MODEL), BF16)],
        compiler_params=pltpu.CompilerParams(dimension_semantics=("arbitrary",),
                                             vmem_limit_bytes=VMEM_LIMIT),
        name="mixout",
    )(h, g, o_attn, y_conv, w_g, w_ao, w_co, w_out)


def kernel(x, p, g_ff1, w_ff1_in, w_ff1_out, g_mix, w_in, b_f, w_attn_out, conv_w, conv_b, g_conv,
           w_conv_out, w_out, g_ff2, w_ff2_in, w_ff2_out, g_ple, w_ple_gate, w_ple_proj, g_final):
    assert x.shape == (BATCH, SEQ, D_MODEL) and p.shape == (DEPTH, BATCH, SEQ, D_PLE)
    m = BATCH * SEQ
    row3 = lambda a: a[:, None, :]

    w_ff1_in_b, w_ff1_out_b = w_ff1_in.astype(BF16), w_ff1_out.astype(BF16)
    w_ff2_in_b, w_ff2_out_b = w_ff2_in.astype(BF16), w_ff2_out.astype(BF16)
    w_qkv = w_in[:, :, 0:F0].astype(BF16)
    w_ft = jnp.pad(jnp.swapaxes(w_in[:, :, F0:C0], 1, 2),
                   ((0, 0), (0, HEAD_ROWS - N_HEADS), (0, 0))).astype(BF16)
    b_fc = jnp.pad(b_f, ((0, 0), (0, HEAD_ROWS - N_HEADS)))[:, :, None]
    w_c = w_in[:, :, C0:GA0].astype(BF16)
    w_g = w_in[:, :, GA0:].astype(BF16)
    w_ao, w_co, w_o = w_attn_out.astype(BF16), w_conv_out.astype(BF16), w_out.astype(BF16)
    w_pg, w_pp = w_ple_gate.astype(BF16), w_ple_proj.astype(BF16)
    conv_wp = jnp.pad(conv_w, ((0, 0), (0, HALO - CONV_K), (0, 0)))
    p2 = p.reshape(DEPTH, m, D_PLE)
    g_ff1_r, g_mix_r, g_ff2_r, g_ple_r = row3(g_ff1), row3(g_mix), row3(g_ff2), row3(g_ple)
    conv_b_r, g_conv_r = row3(conv_b), row3(g_conv)
    g_final_r = g_final[None, :]

    h = x.reshape(m, D_MODEL)
    for i in range(DEPTH):
        h = _ffn_call(h, g_ff1_r, w_ff1_in_b, w_ff1_out_b, i)
        q, k, v, a, c_t = _proj_call(h.reshape(BATCH, SEQ, D_MODEL), g_mix_r, w_qkv, w_c, w_ft, b_fc, i)
        o_attn = _attn_call(q, k, v, c_t)
        y_conv = _conv_call(a, conv_wp, conv_b_r, g_conv_r, i)
        h = _mix_call(h, g_mix_r, o_attn.reshape(m, ATTN_WIDTH), y_conv.reshape(m, CONV_CH),
                      w_g, w_ao, w_co, w_o, i)
        h = _ffn_call(h, g_ff2_r, w_ff2_in_b, w_ff2_out_b, i,
                      ple=(p2, g_ple_r, w_pg, w_pp),
                      g_final=g_final_r if i == DEPTH - 1 else None)
    return h.reshape(BATCH, SEQ, D_MODEL)
```

```python
import functools

import jax
import jax.numpy as jnp
from jax import lax
from jax.experimental import pallas as pl
from jax.experimental.pallas import tpu as pltpu

D_MODEL = 1024
BATCH = 8
SEQ = 2048
DEPTH = 4
N_HEADS = 8
HEAD_DIM = 64
ATTN_WIDTH = N_HEADS * HEAD_DIM
CONV_CH = 512
CONV_K = 31
D_FF = 2816
D_PLE = 256
EPS = 1e-6
FFN_RES = 0.5

F0 = 3 * ATTN_WIDTH
C0 = F0 + N_HEADS
GA0 = C0 + 2 * CONV_CH

LANES = 128
N_PAIRS = N_HEADS // 2
PIECES = 3
BIAS_LANES = PIECES * N_HEADS
ONES_ROWS = 16
HALO = 32
TM = 512
FFN_CHUNK = 256
MIX_CHUNK = 256
TQ = 256
TK = 256
CONV_ROWS = 64
MASKED = -1e30
LOG2_E = 1.4426950408889634
VMEM_LIMIT = 56 * 1024 * 1024

BF16 = jnp.bfloat16
F32 = jnp.float32


def _dot(a, b):
    return jnp.dot(a, b, preferred_element_type=F32)


def _dot_nt(a, b):
    return lax.dot_general(a, b, (((1,), (1,)), ((), ())), preferred_element_type=F32)


def _rms(x, g):
    ms = jnp.mean(x * x, axis=-1, keepdims=True)
    return x * lax.rsqrt(ms + EPS) * g


def _sigmoid(x):
    return 1.0 / (1.0 + jnp.exp(-x))


def _split3(x):
    hi = x.astype(BF16)
    r1 = x - hi.astype(F32)
    mid = r1.astype(BF16)
    lo = (r1 - mid.astype(F32)).astype(BF16)
    return hi, mid, lo


def _resident(shape):
    nd = len(shape)
    return pl.BlockSpec(shape, lambda *_: (0,) * nd, pipeline_mode=pl.Buffered(1))


def _layer_block(shape, layer):
    nd = len(shape)
    return pl.BlockSpec((None,) + tuple(shape), lambda *_: (layer,) + (0,) * nd,
                        pipeline_mode=pl.Buffered(1))


def _ffn_body(*refs, with_ple, with_final):
    h_ref, g_ref, win_ref, wout_ref = refs[:4]
    pos = 4
    if with_ple:
        p_ref, gple_ref, wpg_ref, wpp_ref = refs[pos:pos + 4]
        pos += 4
    if with_final:
        gfin_ref = refs[pos]
        pos += 1
    o_ref, u_scr, act_scr = refs[pos:pos + 3]

    x = h_ref[...]
    u_scr[...] = _rms(x, g_ref[...]).astype(BF16)
    for c in range(D_FF // FFN_CHUNK):
        lo = c * FFN_CHUNK
        u = u_scr[...]
        a = _dot(u, win_ref[:, lo:lo + FFN_CHUNK])
        b = _dot(u, win_ref[:, D_FF + lo:D_FF + lo + FFN_CHUNK])
        act_scr[:, lo:lo + FFN_CHUNK] = (a * _sigmoid(a) * b).astype(BF16)
    h1 = x + FFN_RES * _dot(act_scr[...], wout_ref[...])
    if with_ple:
        u2 = _rms(h1, gple_ref[...]).astype(BF16)
        gate = _sigmoid(_dot(u2, wpg_ref[...]))
        h1 = h1 + gate * _dot(p_ref[...].astype(BF16), wpp_ref[...])
    if with_final:
        h1 = _rms(h1, gfin_ref[...])
    o_ref[...] = h1


def _ffn_call(h, g, w_in, w_out, layer, ple=None, g_final=None):
    m = h.shape[0]
    row = lambda i: (i, 0)
    in_specs = [
        pl.BlockSpec((TM, D_MODEL), row),
        _layer_block((1, D_MODEL), layer),
        _layer_block((D_MODEL, 2 * D_FF), layer),
        _layer_block((D_FF, D_MODEL), layer),
    ]
    args = [h, g, w_in, w_out]
    if ple is not None:
        p, g_ple, w_pg, w_pp = ple
        in_specs += [
            pl.BlockSpec((None, TM, D_PLE), lambda i: (layer, i, 0)),
            _layer_block((1, D_MODEL), layer),
            _layer_block((D_MODEL, D_MODEL), layer),
            _layer_block((D_PLE, D_MODEL), layer),
        ]
        args += [p, g_ple, w_pg, w_pp]
    if g_final is not None:
        in_specs.append(_resident((1, D_MODEL)))
        args.append(g_final)
    body = functools.partial(_ffn_body, with_ple=ple is not None, with_final=g_final is not None)
    return pl.pallas_call(
        body,
        grid=(m // TM,),
        in_specs=in_specs,
        out_specs=pl.BlockSpec((TM, D_MODEL), row),
        out_shape=jax.ShapeDtypeStruct((m, D_MODEL), F32),
        scratch_shapes=[pltpu.VMEM((TM, D_MODEL), BF16), pltpu.VMEM((TM, D_FF), BF16)],
        compiler_params=pltpu.CompilerParams(dimension_semantics=("arbitrary",),
                                             vmem_limit_bytes=VMEM_LIMIT),
        name="ffn_ple" if ple is not None else "ffn",
    )(*args)


def _proj_body(h_ref, g_ref, wqt_ref, wk_ref, wvt_ref, wc_ref, wf_ref, bf_ref,
               qt_ref, k_ref, vt_ref, a_ref, cb_ref, carry_scr):
    @pl.when(pl.program_id(1) == 0)
    def _():
        carry_scr[...] = jnp.zeros_like(carry_scr)

    u = _rms(h_ref[...], g_ref[...]).astype(BF16)
    scale = LOG2_E / (HEAD_DIM ** 0.5)
    qt_ref[...] = (_dot_nt(wqt_ref[...], u) * scale).astype(BF16)
    vt_ref[...] = _dot_nt(wvt_ref[...], u).astype(BF16)
    k_ref[...] = _dot(u, wk_ref[...]).astype(BF16)
    glu_a = _dot(u, wc_ref[:, 0:CONV_CH])
    glu_b = _dot(u, wc_ref[:, CONV_CH:2 * CONV_CH])
    a_ref[...] = glu_a * _sigmoid(glu_b)

    lane = lax.broadcasted_iota(jnp.int32, (TM, LANES), 1)
    f = _dot(u, wf_ref[...]) + bf_ref[...]
    ls = jnp.minimum(f, 0.0) - jnp.log1p(jnp.exp(-jnp.abs(f)))
    ls = jnp.where(lane < BIAS_LANES, ls, 0.0)
    dst = lax.broadcasted_iota(jnp.int32, (TM, TM), 0)
    src = lax.broadcasted_iota(jnp.int32, (TM, TM), 1)
    tri = jnp.where(src <= dst, 1.0, 0.0).astype(BF16)
    cs3 = _dot(tri, jnp.concatenate(_split3(ls), axis=1))
    c = cs3[:, 0:LANES] + cs3[:, LANES:2 * LANES] + cs3[:, 2 * LANES:3 * LANES] + carry_scr[0:1, :]
    carry_scr[...] = jnp.broadcast_to(c[TM - 1:TM, :], carry_scr.shape)
    hi, mid, lo = _split3(-LOG2_E * c)
    lane1 = lax.broadcasted_iota(jnp.int32, (1, LANES), 1)
    is_hi = functools.reduce(jnp.logical_or, [lane1 == PIECES * h for h in range(N_HEADS)])
    is_mid = functools.reduce(jnp.logical_or, [lane1 == PIECES * h + 1 for h in range(N_HEADS)])
    cb_ref[...] = jnp.where(is_hi, hi, jnp.where(is_mid, mid, lo))


def _proj_call(h3, g, w_qt, w_k, w_vt, w_c, w_f, b_f, layer):
    tile = lambda b, s: (b, s, 0)
    tile_t = lambda b, s: (b, 0, s)
    return pl.pallas_call(
        _proj_body,
        grid=(BATCH, SEQ // TM),
        in_specs=[
            pl.BlockSpec((None, TM, D_MODEL), tile),
            _layer_block((1, D_MODEL), layer),
            _layer_block((ATTN_WIDTH, D_MODEL), layer),
            _layer_block((D_MODEL, ATTN_WIDTH), layer),
            _layer_block((ATTN_WIDTH, D_MODEL), layer),
            _layer_block((D_MODEL, 2 * CONV_CH), layer),
            _layer_block((D_MODEL, LANES), layer),
            _layer_block((1, LANES), layer),
        ],
        out_specs=[
            pl.BlockSpec((None, ATTN_WIDTH, TM), tile_t),
            pl.BlockSpec((None, TM, ATTN_WIDTH), tile),
            pl.BlockSpec((None, ATTN_WIDTH, TM), tile_t),
            pl.BlockSpec((None, TM, CONV_CH), tile),
            pl.BlockSpec((None, TM, LANES), tile),
        ],
        out_shape=[
            jax.ShapeDtypeStruct((BATCH, ATTN_WIDTH, SEQ), BF16),
            jax.ShapeDtypeStruct((BATCH, SEQ, ATTN_WIDTH), BF16),
            jax.ShapeDtypeStruct((BATCH, ATTN_WIDTH, SEQ), BF16),
            jax.ShapeDtypeStruct((BATCH, SEQ, CONV_CH), F32),
            jax.ShapeDtypeStruct((BATCH, SEQ, LANES), BF16),
        ],
        scratch_shapes=[pltpu.VMEM((8, LANES), F32)],
        compiler_params=pltpu.CompilerParams(dimension_semantics=("arbitrary", "arbitrary"),
                                             vmem_limit_bytes=VMEM_LIMIT),
        name="proj",
    )(h3, g, w_qt, w_k, w_vt, w_c, w_f, b_f)


def _attn_body(qt_ref, k_ref, cb_ref, vt_ref, o_ref, qa_scr, s_scr, m_scr, acc_scr):
    i = pl.program_id(1)
    nq = 2 * TQ
    sub = lax.broadcasted_iota(jnp.int32, (LANES, TQ), 0)
    srow = lax.broadcasted_iota(jnp.int32, (LANES, nq), 0)
    scol = lax.broadcasted_iota(jnp.int32, (LANES, nq), 1)
    for g in range(N_PAIRS):
        qt = qt_ref[g * LANES:(g + 1) * LANES, :]
        zero = jnp.zeros_like(qt)
        qa_scr[g, 0:LANES, :] = jnp.concatenate(
            [jnp.where(sub < HEAD_DIM, qt, zero), jnp.where(sub >= HEAD_DIM, qt, zero)], axis=1)
        first = PIECES * jnp.where(scol < TQ, 2 * g, 2 * g + 1)
        sel = (srow >= first) & (srow < first + PIECES)
        qa_scr[g, LANES:2 * LANES, :] = jnp.where(sel, 1.0, 0.0).astype(BF16)
    m_scr[...] = jnp.full(m_scr.shape, MASKED, F32)
    acc_scr[...] = jnp.zeros_like(acc_scr)

    ones = jnp.ones((ONES_ROWS, TK), BF16)

    def scores(j, g):
        k0 = pl.multiple_of(j * TK, TK)
        ka = jnp.concatenate([k_ref[pl.ds(k0, TK), g * LANES:(g + 1) * LANES],
                              cb_ref[pl.ds(k0, TK), :]], axis=1)
        s_scr[g] = _dot(ka, qa_scr[g])

    def softmax_pv(j, g, diagonal):
        k0 = pl.multiple_of(j * TK, TK)
        st = s_scr[g]
        if diagonal:
            key = lax.broadcasted_iota(jnp.int32, (TK, nq), 0)
            qcol = lax.broadcasted_iota(jnp.int32, (TK, nq), 1)
            st = jnp.where(key <= jnp.where(qcol >= TQ, qcol - TQ, qcol), st, MASKED)
        m_old = m_scr[g]
        m_new = jnp.maximum(m_old, jnp.max(st, axis=0, keepdims=True))
        pt = jnp.exp2(st - m_new).astype(BF16)
        va = jnp.concatenate([vt_ref[g * LANES:(g + 1) * LANES, pl.ds(k0, TK)], ones], axis=0)
        acc_scr[g] = jnp.exp2(m_old - m_new) * acc_scr[g] + _dot(va, pt)
        m_scr[g] = m_new

    for g in range(N_PAIRS):
        scores(0, g)

    def body(j, carry):
        for g in range(N_PAIRS):
            softmax_pv(j, g, False)
            scores(j + 1, g)
        return carry

    lax.fori_loop(0, i, body, 0)
    for g in range(N_PAIRS):
        softmax_pv(i, g, True)

    for g in range(N_PAIRS):
        acc = acc_scr[g]
        l = acc[LANES:LANES + 1, :]
        out_t = jnp.concatenate([acc[0:HEAD_DIM, 0:TQ] / l[:, 0:TQ],
                                 acc[HEAD_DIM:LANES, TQ:nq] / l[:, TQ:nq]], axis=0)
        o_ref[:, g * LANES:(g + 1) * LANES] = out_t.T.astype(BF16)


def _attn_call(q_t, k, c_b, v_t):
    assert TQ == TK
    whole = lambda b, i: (b, 0, 0)
    return pl.pallas_call(
        _attn_body,
        grid=(BATCH, SEQ // TQ),
        in_specs=[
            pl.BlockSpec((None, ATTN_WIDTH, TQ), lambda b, i: (b, 0, i)),
            pl.BlockSpec((None, SEQ, ATTN_WIDTH), whole),
            pl.BlockSpec((None, SEQ, LANES), whole),
            pl.BlockSpec((None, ATTN_WIDTH, SEQ), whole),
        ],
        out_specs=pl.BlockSpec((None, TQ, ATTN_WIDTH), lambda b, i: (b, i, 0)),
        out_shape=jax.ShapeDtypeStruct((BATCH, SEQ, ATTN_WIDTH), BF16),
        scratch_shapes=[pltpu.VMEM((N_PAIRS, 2 * LANES, 2 * TQ), BF16),
                        pltpu.VMEM((N_PAIRS, TK, 2 * TQ), F32),
                        pltpu.VMEM((N_PAIRS, 1, 2 * TQ), F32),
                        pltpu.VMEM((N_PAIRS, LANES + ONES_ROWS, 2 * TQ), F32)],
        compiler_params=pltpu.CompilerParams(dimension_semantics=("arbitrary", "arbitrary"),
                                             vmem_limit_bytes=VMEM_LIMIT),
        name="attn",
    )(q_t, k, c_b, v_t)


def _conv_body(a_ref, halo_ref, w_ref, b_ref, g_ref, o_ref, xp_scr):
    first = pl.program_id(1) == 0
    halo = halo_ref[...]
    xp_scr[0:HALO, :] = jnp.where(first, jnp.zeros_like(halo), halo)
    xp_scr[HALO:HALO + TM, :] = a_ref[...]
    shift = HALO - (CONV_K - 1)
    for r in range(TM // CONV_ROWS):
        r0 = r * CONV_ROWS
        acc = jnp.broadcast_to(b_ref[...], (CONV_ROWS, CONV_CH))
        for j in range(CONV_K):
            acc = acc + w_ref[j:j + 1, :] * xp_scr[r0 + shift + j:r0 + shift + j + CONV_ROWS, :]
        y = _rms(acc, g_ref[...])
        o_ref[r0:r0 + CONV_ROWS, :] = (y * _sigmoid(y)).astype(BF16)


def _conv_call(a, conv_w, conv_b, g_conv, layer):
    n_h = TM // HALO
    return pl.pallas_call(
        _conv_body,
        grid=(BATCH, SEQ // TM),
        in_specs=[
            pl.BlockSpec((None, TM, CONV_CH), lambda b, s: (b, s, 0)),
            pl.BlockSpec((None, HALO, CONV_CH), lambda b, s: (b, jnp.maximum(s * n_h - 1, 0), 0)),
            _layer_block((HALO, CONV_CH), layer),
            _layer_block((1, CONV_CH), layer),
            _layer_block((1, CONV_CH), layer),
        ],
        out_specs=pl.BlockSpec((None, TM, CONV_CH), lambda b, s: (b, s, 0)),
        out_shape=jax.ShapeDtypeStruct((BATCH, SEQ, CONV_CH), BF16),
        scratch_shapes=[pltpu.VMEM((HALO + TM, CONV_CH), F32)],
        compiler_params=pltpu.CompilerParams(dimension_semantics=("arbitrary", "arbitrary"),
                                             vmem_limit_bytes=VMEM_LIMIT),
        name="conv",
    )(a, a, conv_w, conv_b, g_conv)


def _mix_body(h_ref, g_ref, oa_ref, yc_ref, wg_ref, wao_ref, wco_ref, wout_ref, o_ref, m_scr):
    x = h_ref[...]
    u = _rms(x, g_ref[...]).astype(BF16)
    oa = oa_ref[...]
    yc = yc_ref[...]
    for c in range(D_MODEL // MIX_CHUNK):
        lo = c * MIX_CHUNK
        ga = _sigmoid(_dot(u, wg_ref[:, lo:lo + MIX_CHUNK]))
        gc = _sigmoid(_dot(u, wg_ref[:, D_MODEL + lo:D_MODEL + lo + MIX_CHUNK]))
        ya = _dot(oa, wao_ref[:, lo:lo + MIX_CHUNK])
        yv = _dot(yc, wco_ref[:, lo:lo + MIX_CHUNK])
        m_scr[:, lo:lo + MIX_CHUNK] = (ga * ya + gc * yv).astype(BF16)
    o_ref[...] = x + _dot(m_scr[...], wout_ref[...])


def _mix_call(h, g, o_attn, y_conv, w_g, w_ao, w_co, w_out, layer):
    m = h.shape[0]
    row = lambda i: (i, 0)
    return pl.pallas_call(
        _mix_body,
        grid=(m // TM,),
        in_specs=[
            pl.BlockSpec((TM, D_MODEL), row),
            _layer_block((1, D_MODEL), layer),
            pl.BlockSpec((TM, ATTN_WIDTH), row),
            pl.BlockSpec((TM, CONV_CH), row),
            _layer_block((D_MODEL, 2 * D_MODEL), layer),
            _layer_block((ATTN_WIDTH, D_MODEL), layer),
            _layer_block((CONV_CH, D_MODEL), layer),
            _layer_block((D_MODEL, D_MODEL), layer),
        ],
        out_specs=pl.BlockSpec((TM, D_MODEL), row),
        out_shape=jax.ShapeDtypeStruct((m, D_MODEL), F32),
        scratch_shapes=[pltpu.VMEM((TM, D_MODEL), BF16)],
        compiler_params=pltpu.CompilerParams(dimension_semantics=("arbitrary",),
                                             vmem_limit_bytes=VMEM_LIMIT),
        name="mixout",
    )(h, g, o_attn, y_conv, w_g, w_ao, w_co, w_out)


def kernel(x, p, g_ff1, w_ff1_in, w_ff1_out, g_mix, w_in, b_f, w_attn_out, conv_w, conv_b, g_conv,
           w_conv_out, w_out, g_ff2, w_ff2_in, w_ff2_out, g_ple, w_ple_gate, w_ple_proj, g_final):
    assert x.shape == (BATCH, SEQ, D_MODEL) and p.shape == (DEPTH, BATCH, SEQ, D_PLE)
    m = BATCH * SEQ
    row3 = lambda a: a[:, None, :]

    w_ff1_in_b, w_ff1_out_b = w_ff1_in.astype(BF16), w_ff1_out.astype(BF16)
    w_ff2_in_b, w_ff2_out_b = w_ff2_in.astype(BF16), w_ff2_out.astype(BF16)
    w_qt = jnp.swapaxes(w_in[:, :, 0:ATTN_WIDTH], 1, 2).astype(BF16)
    w_k = w_in[:, :, ATTN_WIDTH:2 * ATTN_WIDTH].astype(BF16)
    w_vt = jnp.swapaxes(w_in[:, :, 2 * ATTN_WIDTH:F0], 1, 2).astype(BF16)
    rep = jnp.pad(jnp.repeat(w_in[:, :, F0:C0], PIECES, axis=2),
                  ((0, 0), (0, 0), (0, LANES - BIAS_LANES))).astype(BF16)
    b_rep = jnp.pad(jnp.repeat(b_f, PIECES, axis=1), ((0, 0), (0, LANES - BIAS_LANES)))[:, None, :]
    w_c = w_in[:, :, C0:GA0].astype(BF16)
    w_g = w_in[:, :, GA0:].astype(BF16)
    w_ao, w_co, w_o = w_attn_out.astype(BF16), w_conv_out.astype(BF16), w_out.astype(BF16)
    w_pg, w_pp = w_ple_gate.astype(BF16), w_ple_proj.astype(BF16)
    conv_wp = jnp.pad(conv_w, ((0, 0), (0, HALO - CONV_K), (0, 0)))
    p2 = p.reshape(DEPTH, m, D_PLE)
    g_ff1_r, g_mix_r, g_ff2_r, g_ple_r = row3(g_ff1), row3(g_mix), row3(g_ff2), row3(g_ple)
    conv_b_r, g_conv_r = row3(conv_b), row3(g_conv)
    g_final_r = g_final[None, :]

    h = x.reshape(m, D_MODEL)
    for i in range(DEPTH):
        h = _ffn_call(h, g_ff1_r, w_ff1_in_b, w_ff1_out_b, i)
        q_t, k, v_t, a, c_b = _proj_call(h.reshape(BATCH, SEQ, D_MODEL), g_mix_r,
                                         w_qt, w_k, w_vt, w_c, rep, b_rep, i)
        o_attn = _attn_call(q_t, k, c_b, v_t)
        y_conv = _conv_call(a, conv_wp, conv_b_r, g_conv_r, i)
        h = _mix_call(h, g_mix_r, o_attn.reshape(m, ATTN_WIDTH), y_conv.reshape(m, CONV_CH),
                      w_g, w_ao, w_co, w_o, i)
        h = _ffn_call(h, g_ff2_r, w_ff2_in_b, w_ff2_out_b, i,
                      ple=(p2, g_ple_r, w_pg, w_pp),
                      g_final=g_final_r if i == DEPTH - 1 else None)
    return h.reshape(BATCH, SEQ, D_MODEL)
```

```python
import functools

import jax
import jax.numpy as jnp
from jax import lax
from jax.experimental import pallas as pl
from jax.experimental.pallas import tpu as pltpu

D_MODEL = 1024
BATCH = 8
SEQ = 2048
DEPTH = 4
N_HEADS = 8
HEAD_DIM = 64
ATTN_WIDTH = N_HEADS * HEAD_DIM
CONV_CH = 512
CONV_K = 31
D_FF = 2816
D_PLE = 256
EPS = 1e-6
FFN_RES = 0.5

F0 = 3 * ATTN_WIDTH
C0 = F0 + N_HEADS
GA0 = C0 + 2 * CONV_CH

LANES = 128
N_PAIRS = N_HEADS // 2
PIECES = 3
BIAS_LANES = PIECES * N_HEADS
ONES_ROWS = 16
HALO = 32
TM = 512
FFN_CHUNK = 256
MIX_CHUNK = 256
TQ = 256
TK = 256
PROJ_STEPS = 4
CONV_ROWS = 64
CONV_COLS = 128
MASKED = -1e30
LOG2_E = 1.4426950408889634
VMEM_LIMIT = 56 * 1024 * 1024

BF16 = jnp.bfloat16
F32 = jnp.float32


def _dot(a, b):
    return jnp.dot(a, b, preferred_element_type=F32)


def _dot_nt(a, b):
    return lax.dot_general(a, b, (((1,), (1,)), ((), ())), preferred_element_type=F32)


def _rms(x, g):
    ms = jnp.mean(x * x, axis=-1, keepdims=True)
    return x * lax.rsqrt(ms + EPS) * g


def _sigmoid(x):
    return 1.0 / (1.0 + jnp.exp(-x))


def _split3(x):
    hi = x.astype(BF16)
    r1 = x - hi.astype(F32)
    mid = r1.astype(BF16)
    lo = (r1 - mid.astype(F32)).astype(BF16)
    return hi, mid, lo


def _resident(shape):
    nd = len(shape)
    return pl.BlockSpec(shape, lambda *_: (0,) * nd, pipeline_mode=pl.Buffered(1))


def _layer_block(shape, layer):
    nd = len(shape)
    return pl.BlockSpec((None,) + tuple(shape), lambda *_: (layer,) + (0,) * nd,
                        pipeline_mode=pl.Buffered(1))


def _ffn_body(*refs, with_ple, with_final):
    h_ref, g_ref, win_ref, wout_ref = refs[:4]
    pos = 4
    if with_ple:
        p_ref, gple_ref, wpg_ref, wpp_ref = refs[pos:pos + 4]
        pos += 4
    if with_final:
        gfin_ref = refs[pos]
        pos += 1
    o_ref, u_scr, act_scr = refs[pos:pos + 3]

    x = h_ref[...]
    u_scr[...] = _rms(x, g_ref[...]).astype(BF16)
    for c in range(D_FF // FFN_CHUNK):
        lo = c * FFN_CHUNK
        u = u_scr[...]
        a = _dot(u, win_ref[:, lo:lo + FFN_CHUNK])
        b = _dot(u, win_ref[:, D_FF + lo:D_FF + lo + FFN_CHUNK])
        act_scr[:, lo:lo + FFN_CHUNK] = (a * _sigmoid(a) * b).astype(BF16)
    h1 = x + FFN_RES * _dot(act_scr[...], wout_ref[...])
    if with_ple:
        u2 = _rms(h1, gple_ref[...]).astype(BF16)
        gate = _sigmoid(_dot(u2, wpg_ref[...]))
        h1 = h1 + gate * _dot(p_ref[...].astype(BF16), wpp_ref[...])
    if with_final:
        h1 = _rms(h1, gfin_ref[...])
    o_ref[...] = h1


def _ffn_call(h, g, w_in, w_out, layer, ple=None, g_final=None):
    m = h.shape[0]
    row = lambda i: (i, 0)
    in_specs = [
        pl.BlockSpec((TM, D_MODEL), row),
        _layer_block((1, D_MODEL), layer),
        _layer_block((D_MODEL, 2 * D_FF), layer),
        _layer_block((D_FF, D_MODEL), layer),
    ]
    args = [h, g, w_in, w_out]
    if ple is not None:
        p, g_ple, w_pg, w_pp = ple
        in_specs += [
            pl.BlockSpec((None, TM, D_PLE), lambda i: (layer, i, 0)),
            _layer_block((1, D_MODEL), layer),
            _layer_block((D_MODEL, D_MODEL), layer),
            _layer_block((D_PLE, D_MODEL), layer),
        ]
        args += [p, g_ple, w_pg, w_pp]
    if g_final is not None:
        in_specs.append(_resident((1, D_MODEL)))
        args.append(g_final)
    body = functools.partial(_ffn_body, with_ple=ple is not None, with_final=g_final is not None)
    return pl.pallas_call(
        body,
        grid=(m // TM,),
        in_specs=in_specs,
        out_specs=pl.BlockSpec((TM, D_MODEL), row),
        out_shape=jax.ShapeDtypeStruct((m, D_MODEL), F32),
        scratch_shapes=[pltpu.VMEM((TM, D_MODEL), BF16), pltpu.VMEM((TM, D_FF), BF16)],
        compiler_params=pltpu.CompilerParams(dimension_semantics=("arbitrary",),
                                             vmem_limit_bytes=VMEM_LIMIT),
        name="ffn_ple" if ple is not None else "ffn",
    )(*args)


def _conv_module(xp_scr, y_scr, cw_ref, cb_ref, gc_ref, yc_ref):
    sub = 8

    def conv_rows(row0):
        for c0 in range(0, CONV_CH, CONV_COLS):
            cols = slice(c0, c0 + CONV_COLS)
            acc = jnp.broadcast_to(cb_ref[:, cols], (CONV_ROWS, CONV_COLS))
            for f in range(sub):
                gf = None
                for d in range(f, CONV_K, sub):
                    lo = pl.multiple_of(row0 + (HALO - sub - (d - f)), sub)
                    x = xp_scr[pl.ds(lo, CONV_ROWS + sub), cols].reshape(-1, sub, CONV_COLS)
                    term = x * cw_ref[CONV_K - 1 - d, :, cols]
                    gf = term if gf is None else gf + term
                acc = acc + gf.reshape(CONV_ROWS + sub, CONV_COLS)[sub - f:sub - f + CONV_ROWS]
            y_scr[:, cols] = acc
        y = _rms(y_scr[...], gc_ref[...])
        yc_ref[pl.ds(row0, CONV_ROWS), :] = (y * _sigmoid(y)).astype(BF16)

    return conv_rows


def _proj_body(h_ref, g_ref, wqvt_ref, wk_ref, wc_ref, wf_ref, bf_ref, cw_ref, cbias_ref, gc_ref,
               qvt_ref, k_ref, yc_ref, cb_ref, carry_scr, u_scr, xp_scr, y_scr):
    @pl.when(pl.program_id(1) == 0)
    def _():
        carry_scr[...] = jnp.zeros_like(carry_scr)
        xp_scr[0:HALO, :] = jnp.zeros((HALO, CONV_CH), F32)

    u_scr[...] = _rms(h_ref[...], g_ref[...]).astype(BF16)
    glu_a = _dot(u_scr[...], wc_ref[:, 0:CONV_CH])
    glu_b = _dot(u_scr[...], wc_ref[:, CONV_CH:2 * CONV_CH])
    xp_scr[HALO:HALO + TM, :] = glu_a * _sigmoid(glu_b)

    scale = LOG2_E / (HEAD_DIM ** 0.5)
    conv_rows = _conv_module(xp_scr, y_scr, cw_ref, cbias_ref, gc_ref, yc_ref)
    qv_rows = 2 * ATTN_WIDTH // PROJ_STEPS
    rows = TM // PROJ_STEPS
    lane1 = lax.broadcasted_iota(jnp.int32, (1, LANES), 1)
    is_hi = functools.reduce(jnp.logical_or, [lane1 == PIECES * h for h in range(N_HEADS)])
    is_mid = functools.reduce(jnp.logical_or, [lane1 == PIECES * h + 1 for h in range(N_HEADS)])

    def step(r, carry):
        row0 = pl.multiple_of(r * rows, rows)
        for sub_block in range(rows // CONV_ROWS):
            conv_rows(row0 + sub_block * CONV_ROWS)

        w0 = pl.multiple_of(r * qv_rows, qv_rows)
        row_scale = jnp.where(w0 < ATTN_WIDTH, scale, 1.0)
        qvt_ref[pl.ds(w0, qv_rows), :] = (
            _dot_nt(wqvt_ref[pl.ds(w0, qv_rows), :], u_scr[...]) * row_scale).astype(BF16)

        u_rows = u_scr[pl.ds(row0, rows), :]
        k_ref[pl.ds(row0, rows), :] = _dot(u_rows, wk_ref[...]).astype(BF16)

        f = _dot(u_rows, wf_ref[...]) + bf_ref[...]
        ls = jnp.minimum(f, 0.0) - jnp.log1p(jnp.exp(-jnp.abs(f)))
        ls = jnp.where(lane1 < BIAS_LANES, ls, 0.0)
        dst = lax.broadcasted_iota(jnp.int32, (rows, rows), 0)
        src = lax.broadcasted_iota(jnp.int32, (rows, rows), 1)
        tri = jnp.where(src <= dst, 1.0, 0.0).astype(BF16)
        cs3 = _dot(tri, jnp.concatenate(_split3(ls), axis=1))
        c = cs3[:, 0:LANES] + cs3[:, LANES:2 * LANES] + cs3[:, 2 * LANES:3 * LANES] + carry_scr[0:1, :]
        carry_scr[...] = jnp.broadcast_to(c[rows - 1:rows, :], carry_scr.shape)
        hi, mid, lo = _split3(-LOG2_E * c)
        cb_ref[pl.ds(row0, rows), :] = jnp.where(is_hi, hi, jnp.where(is_mid, mid, lo))
        return carry

    lax.fori_loop(0, PROJ_STEPS, step, 0)
    xp_scr[0:HALO, :] = xp_scr[TM:TM + HALO, :]


def _proj_call(h3, g, w_qvt, w_k, w_c, w_f, b_f, conv_w, conv_b, g_conv, layer):
    assert TM % (PROJ_STEPS * CONV_ROWS) == 0
    tile = lambda b, s: (b, s, 0)
    return pl.pallas_call(
        _proj_body,
        grid=(BATCH, SEQ // TM),
        in_specs=[
            pl.BlockSpec((None, TM, D_MODEL), tile),
            _layer_block((1, D_MODEL), layer),
            _layer_block((2 * ATTN_WIDTH, D_MODEL), layer),
            _layer_block((D_MODEL, ATTN_WIDTH), layer),
            _layer_block((D_MODEL, 2 * CONV_CH), layer),
            _layer_block((D_MODEL, LANES), layer),
            _layer_block((1, LANES), layer),
            _layer_block((CONV_K, 8, CONV_CH), layer),
            _layer_block((1, CONV_CH), layer),
            _layer_block((1, CONV_CH), layer),
        ],
        out_specs=[
            pl.BlockSpec((None, 2 * ATTN_WIDTH, TM), lambda b, s: (b, 0, s)),
            pl.BlockSpec((None, TM, ATTN_WIDTH), tile),
            pl.BlockSpec((None, TM, CONV_CH), tile),
            pl.BlockSpec((None, TM, LANES), tile),
        ],
        out_shape=[
            jax.ShapeDtypeStruct((BATCH, 2 * ATTN_WIDTH, SEQ), BF16),
            jax.ShapeDtypeStruct((BATCH, SEQ, ATTN_WIDTH), BF16),
            jax.ShapeDtypeStruct((BATCH, SEQ, CONV_CH), BF16),
            jax.ShapeDtypeStruct((BATCH, SEQ, LANES), BF16),
        ],
        scratch_shapes=[pltpu.VMEM((8, LANES), F32),
                        pltpu.VMEM((TM, D_MODEL), BF16),
                        pltpu.VMEM((HALO + TM, CONV_CH), F32),
                        pltpu.VMEM((CONV_ROWS, CONV_CH), F32)],
        compiler_params=pltpu.CompilerParams(dimension_semantics=("arbitrary", "arbitrary"),
                                             vmem_limit_bytes=VMEM_LIMIT),
        name="proj",
    )(h3, g, w_qvt, w_k, w_c, w_f, b_f, conv_w, conv_b, g_conv)


def _attn_body(qt_ref, k_ref, cb_ref, vt_ref, o_ref, qa_scr, s_scr, m_scr, acc_scr):
    i = pl.program_id(1)
    nq = 2 * TQ
    sub = lax.broadcasted_iota(jnp.int32, (LANES, TQ), 0)
    srow = lax.broadcasted_iota(jnp.int32, (LANES, nq), 0)
    scol = lax.broadcasted_iota(jnp.int32, (LANES, nq), 1)
    for g in range(N_PAIRS):
        qt = qt_ref[g * LANES:(g + 1) * LANES, :]
        zero = jnp.zeros_like(qt)
        qa_scr[g, 0:LANES, :] = jnp.concatenate(
            [jnp.where(sub < HEAD_DIM, qt, zero), jnp.where(sub >= HEAD_DIM, qt, zero)], axis=1)
        first = PIECES * jnp.where(scol < TQ, 2 * g, 2 * g + 1)
        sel = (srow >= first) & (srow < first + PIECES)
        qa_scr[g, LANES:2 * LANES, :] = jnp.where(sel, 1.0, 0.0).astype(BF16)
    m_scr[...] = jnp.full(m_scr.shape, MASKED, F32)
    acc_scr[...] = jnp.zeros_like(acc_scr)

    ones = jnp.ones((ONES_ROWS, TK), BF16)

    def scores(j, g):
        k0 = pl.multiple_of(j * TK, TK)
        ka = jnp.concatenate([k_ref[pl.ds(k0, TK), g * LANES:(g + 1) * LANES],
                              cb_ref[pl.ds(k0, TK), :]], axis=1)
        s_scr[g] = _dot(ka, qa_scr[g])

    def softmax_pv(j, g, diagonal):
        k0 = pl.multiple_of(j * TK, TK)
        st = s_scr[g]
        if diagonal:
            key = lax.broadcasted_iota(jnp.int32, (TK, nq), 0)
            qcol = lax.broadcasted_iota(jnp.int32, (TK, nq), 1)
            st = jnp.where(key <= jnp.where(qcol >= TQ, qcol - TQ, qcol), st, MASKED)
        m_old = m_scr[g]
        m_new = jnp.maximum(m_old, jnp.max(st, axis=0, keepdims=True))
        pt = jnp.exp2(st - m_new).astype(BF16)
        va = jnp.concatenate([vt_ref[g * LANES:(g + 1) * LANES, pl.ds(k0, TK)], ones], axis=0)
        acc_scr[g] = jnp.exp2(m_old - m_new) * acc_scr[g] + _dot(va, pt)
        m_scr[g] = m_new

    for g in range(N_PAIRS):
        scores(0, g)

    def body(j, carry):
        for g in range(N_PAIRS):
            softmax_pv(j, g, False)
            scores(j + 1, g)
        return carry

    lax.fori_loop(0, i, body, 0)
    for g in range(N_PAIRS):
        softmax_pv(i, g, True)

    for g in range(N_PAIRS):
        acc = acc_scr[g]
        l = acc[LANES:LANES + 1, :]
        out_t = jnp.concatenate([acc[0:HEAD_DIM, 0:TQ] / l[:, 0:TQ],
                                 acc[HEAD_DIM:LANES, TQ:nq] / l[:, TQ:nq]], axis=0)
        o_ref[:, g * LANES:(g + 1) * LANES] = out_t.T.astype(BF16)


def _attn_call(qv_t, k, c_b):
    assert TQ == TK
    whole = lambda b, i: (b, 0, 0)
    return pl.pallas_call(
        _attn_body,
        grid=(BATCH, SEQ // TQ),
        in_specs=[
            pl.BlockSpec((None, ATTN_WIDTH, TQ), lambda b, i: (b, 0, i)),
            pl.BlockSpec((None, SEQ, ATTN_WIDTH), whole),
            pl.BlockSpec((None, SEQ, LANES), whole),
            pl.BlockSpec((None, ATTN_WIDTH, SEQ), lambda b, i: (b, 1, 0)),
        ],
        out_specs=pl.BlockSpec((None, TQ, ATTN_WIDTH), lambda b, i: (b, i, 0)),
        out_shape=jax.ShapeDtypeStruct((BATCH, SEQ, ATTN_WIDTH), BF16),
        scratch_shapes=[pltpu.VMEM((N_PAIRS, 2 * LANES, 2 * TQ), BF16),
                        pltpu.VMEM((N_PAIRS, TK, 2 * TQ), F32),
                        pltpu.VMEM((N_PAIRS, 1, 2 * TQ), F32),
                        pltpu.VMEM((N_PAIRS, LANES + ONES_ROWS, 2 * TQ), F32)],
        compiler_params=pltpu.CompilerParams(dimension_semantics=("arbitrary", "arbitrary"),
                                             vmem_limit_bytes=VMEM_LIMIT),
        name="attn",
    )(qv_t, k, c_b, qv_t)


def _mix_body(h_ref, g_ref, oa_ref, yc_ref, wg_ref, wao_ref, wco_ref, wout_ref, o_ref, m_scr):
    x = h_ref[...]
    u = _rms(x, g_ref[...]).astype(BF16)
    oa = oa_ref[...]
    yc = yc_ref[...]
    for c in range(D_MODEL // MIX_CHUNK):
        lo = c * MIX_CHUNK
        ga = _sigmoid(_dot(u, wg_ref[:, lo:lo + MIX_CHUNK]))
        gc = _sigmoid(_dot(u, wg_ref[:, D_MODEL + lo:D_MODEL + lo + MIX_CHUNK]))
        ya = _dot(oa, wao_ref[:, lo:lo + MIX_CHUNK])
        yv = _dot(yc, wco_ref[:, lo:lo + MIX_CHUNK])
        m_scr[:, lo:lo + MIX_CHUNK] = (ga * ya + gc * yv).astype(BF16)
    o_ref[...] = x + _dot(m_scr[...], wout_ref[...])


def _mix_call(h, g, o_attn, y_conv, w_g, w_ao, w_co, w_out, layer):
    m = h.shape[0]
    row = lambda i: (i, 0)
    return pl.pallas_call(
        _mix_body,
        grid=(m // TM,),
        in_specs=[
            pl.BlockSpec((TM, D_MODEL), row),
            _layer_block((1, D_MODEL), layer),
            pl.BlockSpec((TM, ATTN_WIDTH), row),
            pl.BlockSpec((TM, CONV_CH), row),
            _layer_block((D_MODEL, 2 * D_MODEL), layer),
            _layer_block((ATTN_WIDTH, D_MODEL), layer),
            _layer_block((CONV_CH, D_MODEL), layer),
            _layer_block((D_MODEL, D_MODEL), layer),
        ],
        out_specs=pl.BlockSpec((TM, D_MODEL), row),
        out_shape=jax.ShapeDtypeStruct((m, D_MODEL), F32),
        scratch_shapes=[pltpu.VMEM((TM, D_MODEL), BF16)],
        compiler_params=pltpu.CompilerParams(dimension_semantics=("arbitrary",),
                                             vmem_limit_bytes=VMEM_LIMIT),
        name="mixout",
    )(h, g, o_attn, y_conv, w_g, w_ao, w_co, w_out)


def kernel(x, p, g_ff1, w_ff1_in, w_ff1_out, g_mix, w_in, b_f, w_attn_out, conv_w, conv_b, g_conv,
           w_conv_out, w_out, g_ff2, w_ff2_in, w_ff2_out, g_ple, w_ple_gate, w_ple_proj, g_final):
    assert x.shape == (BATCH, SEQ, D_MODEL) and p.shape == (DEPTH, BATCH, SEQ, D_PLE)
    m = BATCH * SEQ
    row3 = lambda a: a[:, None, :]

    w_ff1_in_b, w_ff1_out_b = w_ff1_in.astype(BF16), w_ff1_out.astype(BF16)
    w_ff2_in_b, w_ff2_out_b = w_ff2_in.astype(BF16), w_ff2_out.astype(BF16)
    w_qvt = jnp.swapaxes(jnp.concatenate([w_in[:, :, 0:ATTN_WIDTH], w_in[:, :, 2 * ATTN_WIDTH:F0]], axis=2),
                         1, 2).astype(BF16)
    w_k = w_in[:, :, ATTN_WIDTH:2 * ATTN_WIDTH].astype(BF16)
    rep = jnp.pad(jnp.repeat(w_in[:, :, F0:C0], PIECES, axis=2),
                  ((0, 0), (0, 0), (0, LANES - BIAS_LANES))).astype(BF16)
    b_rep = jnp.pad(jnp.repeat(b_f, PIECES, axis=1), ((0, 0), (0, LANES - BIAS_LANES)))[:, None, :]
    w_c = w_in[:, :, C0:GA0].astype(BF16)
    w_g = w_in[:, :, GA0:].astype(BF16)
    w_ao, w_co, w_o = w_attn_out.astype(BF16), w_conv_out.astype(BF16), w_out.astype(BF16)
    w_pg, w_pp = w_ple_gate.astype(BF16), w_ple_proj.astype(BF16)
    conv_wp = jnp.broadcast_to(conv_w[:, :, None, :], (DEPTH, CONV_K, 8, CONV_CH))
    p2 = p.reshape(DEPTH, m, D_PLE)
    g_ff1_r, g_mix_r, g_ff2_r, g_ple_r = row3(g_ff1), row3(g_mix), row3(g_ff2), row3(g_ple)
    conv_b_r, g_conv_r = row3(conv_b), row3(g_conv)
    g_final_r = g_final[None, :]

    h = x.reshape(m, D_MODEL)
    for i in range(DEPTH):
        h = _ffn_call(h, g_ff1_r, w_ff1_in_b, w_ff1_out_b, i)
        qv_t, k, y_conv, c_b = _proj_call(h.reshape(BATCH, SEQ, D_MODEL), g_mix_r,
                                          w_qvt, w_k, w_c, rep, b_rep, conv_wp, conv_b_r, g_conv_r, i)
        o_attn = _attn_call(qv_t, k, c_b)
        h = _mix_call(h, g_mix_r, o_attn.reshape(m, ATTN_WIDTH), y_conv.reshape(m, CONV_CH),
                      w_g, w_ao, w_co, w_o, i)
        h = _ffn_call(h, g_ff2_r, w_ff2_in_b, w_ff2_out_b, i,
                      ple=(p2, g_ple_r, w_pg, w_pp),
                      g_final=g_final_r if i == DEPTH - 1 else None)
    return h.reshape(BATCH, SEQ, D_MODEL)
```

```python
import functools

import jax
import jax.numpy as jnp
from jax import lax
from jax.experimental import pallas as pl
from jax.experimental.pallas import tpu as pltpu

D_MODEL = 1024
BATCH = 8
SEQ = 2048
DEPTH = 4
N_HEADS = 8
HEAD_DIM = 64
ATTN_WIDTH = N_HEADS * HEAD_DIM
CONV_CH = 512
CONV_K = 31
D_FF = 2816
D_PLE = 256
EPS = 1e-6
FFN_RES = 0.5

F0 = 3 * ATTN_WIDTH
C0 = F0 + N_HEADS
GA0 = C0 + 2 * CONV_CH

LANES = 128
N_PAIRS = N_HEADS // 2
PIECES = 3
BIAS_LANES = PIECES * N_HEADS
ONES_ROWS = 16
HALO = 32
TM = 512
FFN_CHUNK = 256
MIX_CHUNK = 256
TQ = 256
TK = 256
GROUP = TM // 8
XS_PITCH = HALO + GROUP + 4
YS_PITCH = GROUP + 4
CONV_ROWS = 16
MASKED = -1e30
LOG2_E = 1.4426950408889634
VMEM_LIMIT = 56 * 1024 * 1024

BF16 = jnp.bfloat16
F32 = jnp.float32


def _dot(a, b):
    return jnp.dot(a, b, preferred_element_type=F32)


def _dot_nt(a, b):
    return lax.dot_general(a, b, (((1,), (1,)), ((), ())), preferred_element_type=F32)


def _rms(x, g):
    ms = jnp.mean(x * x, axis=-1, keepdims=True)
    return x * lax.rsqrt(ms + EPS) * g


def _sigmoid(x):
    return 1.0 / (1.0 + jnp.exp(-x))


def _split3(x):
    hi = x.astype(BF16)
    r1 = x - hi.astype(F32)
    mid = r1.astype(BF16)
    lo = (r1 - mid.astype(F32)).astype(BF16)
    return hi, mid, lo


def _resident(shape):
    nd = len(shape)
    return pl.BlockSpec(shape, lambda *_: (0,) * nd, pipeline_mode=pl.Buffered(1))


def _layer_block(shape, layer):
    nd = len(shape)
    return pl.BlockSpec((None,) + tuple(shape), lambda *_: (layer,) + (0,) * nd,
                        pipeline_mode=pl.Buffered(1))


def _swiglu_half_step(x, g_ref, win_ref, wout_ref, u_scr, act_scr):
    u_scr[...] = _rms(x, g_ref[...]).astype(BF16)
    for c in range(D_FF // FFN_CHUNK):
        lo = c * FFN_CHUNK
        u = u_scr[...]
        a = _dot(u, win_ref[:, lo:lo + FFN_CHUNK])
        b = _dot(u, win_ref[:, D_FF + lo:D_FF + lo + FFN_CHUNK])
        act_scr[:, lo:lo + FFN_CHUNK] = (a * _sigmoid(a) * b).astype(BF16)
    return x + FFN_RES * _dot(act_scr[...], wout_ref[...])


def _ffn_ple_body(*refs, with_final):
    h_ref, g_ref, win_ref, wout_ref, p_ref, gple_ref, wpg_ref, wpp_ref = refs[:8]
    pos = 8
    if with_final:
        gfin_ref = refs[pos]
        pos += 1
    o_ref, u_scr, act_scr = refs[pos:pos + 3]

    h1 = _swiglu_half_step(h_ref[...], g_ref, win_ref, wout_ref, u_scr, act_scr)
    u2 = _rms(h1, gple_ref[...]).astype(BF16)
    gate = _sigmoid(_dot(u2, wpg_ref[...]))
    h1 = h1 + gate * _dot(p_ref[...].astype(BF16), wpp_ref[...])
    if with_final:
        h1 = _rms(h1, gfin_ref[...])
    o_ref[...] = h1


def _ffn_ple_call(h, g, w_in, w_out, p, g_ple, w_pg, w_pp, layer, g_final=None):
    m = h.shape[0]
    row = lambda i: (i, 0)
    in_specs = [
        pl.BlockSpec((TM, D_MODEL), row),
        _layer_block((1, D_MODEL), layer),
        _layer_block((D_MODEL, 2 * D_FF), layer),
        _layer_block((D_FF, D_MODEL), layer),
        pl.BlockSpec((None, TM, D_PLE), lambda i: (layer, i, 0)),
        _layer_block((1, D_MODEL), layer),
        _layer_block((D_MODEL, D_MODEL), layer),
        _layer_block((D_PLE, D_MODEL), layer),
    ]
    args = [h, g, w_in, w_out, p, g_ple, w_pg, w_pp]
    if g_final is not None:
        in_specs.append(_resident((1, D_MODEL)))
        args.append(g_final)
    return pl.pallas_call(
        functools.partial(_ffn_ple_body, with_final=g_final is not None),
        grid=(m // TM,),
        in_specs=in_specs,
        out_specs=pl.BlockSpec((TM, D_MODEL), row),
        out_shape=jax.ShapeDtypeStruct((m, D_MODEL), F32),
        scratch_shapes=[pltpu.VMEM((TM, D_MODEL), BF16), pltpu.VMEM((TM, D_FF), BF16)],
        compiler_params=pltpu.CompilerParams(dimension_semantics=("arbitrary",),
                                             vmem_limit_bytes=VMEM_LIMIT),
        name="ffn_ple",
    )(*args)


def _stage_conv_input(av, tail_prev, xs_scr):
    for s in range(CONV_CH // LANES):
        cols = slice(s * LANES, (s + 1) * LANES)
        for b in range(TM // GROUP):
            base = b * XS_PITCH
            hist = tail_prev[:, cols] if b == 0 else av[b * GROUP - HALO:b * GROUP, cols]
            xs_scr[s, base:base + HALO, :] = hist
            xs_scr[s, base + HALO:base + HALO + GROUP, :] = av[b * GROUP:(b + 1) * GROUP, cols]


def _conv_module(xs_scr, ys_scr, cw_ref, cb_ref, gc_ref, yc_ref):
    n_groups = TM // GROUP
    for s in range(CONV_CH // LANES):
        cols = slice(s * LANES, (s + 1) * LANES)
        bias = jnp.broadcast_to(cb_ref[:, cols], (n_groups, LANES))
        for a0 in range(0, GROUP, CONV_ROWS):
            accs = [bias] * CONV_ROWS
            for d in range(CONV_K):
                w = cw_ref[CONV_K - 1 - d, :, cols]
                for j in range(CONV_ROWS):
                    x = xs_scr[s, pl.ds(HALO + a0 + j - d, n_groups, stride=XS_PITCH), :]
                    accs[j] = accs[j] + w * x
            for j in range(CONV_ROWS):
                ys_scr[s, pl.ds(a0 + j, n_groups, stride=YS_PITCH), :] = accs[j]
    for b in range(n_groups):
        y = jnp.concatenate([ys_scr[s, b * YS_PITCH:b * YS_PITCH + GROUP, :]
                             for s in range(CONV_CH // LANES)], axis=1)
        y = _rms(y, gc_ref[...])
        yc_ref[b * GROUP:(b + 1) * GROUP, :] = (y * _sigmoid(y)).astype(BF16)


def _ffn_proj_body(h_ref, gff_ref, win_ref, wout_ref, gmix_ref, wqvt_ref, wk_ref, wc_ref, wf_ref, bf_ref,
                   cw_ref, cbias_ref, gc_ref,
                   h1_ref, qvt_ref, k_ref, cb_ref, yc_ref,
                   u_scr, act_scr, u2_scr, xs_scr, ys_scr, tail_scr, carry_scr, *, n_tiles):
    i = pl.program_id(0)

    @pl.when(i == 0)
    def _():
        xs_scr[...] = jnp.zeros_like(xs_scr)
        ys_scr[...] = jnp.zeros_like(ys_scr)
        tail_scr[...] = jnp.zeros_like(tail_scr)
        carry_scr[...] = jnp.zeros_like(carry_scr)

    @pl.when(i < n_tiles)
    def _():
        _conv_module(xs_scr, ys_scr, cw_ref, cbias_ref, gc_ref, yc_ref)

        h1 = _swiglu_half_step(h_ref[...], gff_ref, win_ref, wout_ref, u_scr, act_scr)
        h1_ref[...] = h1

        first = (i % (SEQ // TM)) == 0
        u2_scr[...] = _rms(h1, gmix_ref[...]).astype(BF16)
        glu_a = _dot(u2_scr[...], wc_ref[:, 0:CONV_CH])
        glu_b = _dot(u2_scr[...], wc_ref[:, CONV_CH:2 * CONV_CH])
        conv_in = glu_a * _sigmoid(glu_b)
        _stage_conv_input(conv_in, jnp.where(first, 0.0, tail_scr[...]), xs_scr)
        tail_scr[...] = conv_in[TM - HALO:TM, :]

        scale = LOG2_E / (HEAD_DIM ** 0.5)
        qvt_ref[0:ATTN_WIDTH, :] = (_dot_nt(wqvt_ref[0:ATTN_WIDTH, :], u2_scr[...]) * scale).astype(BF16)
        qvt_ref[ATTN_WIDTH:2 * ATTN_WIDTH, :] = _dot_nt(wqvt_ref[ATTN_WIDTH:2 * ATTN_WIDTH, :],
                                                        u2_scr[...]).astype(BF16)
        k_ref[...] = _dot(u2_scr[...], wk_ref[...]).astype(BF16)

        lane1 = lax.broadcasted_iota(jnp.int32, (1, LANES), 1)
        f = _dot(u2_scr[...], wf_ref[...]) + bf_ref[...]
        ls = jnp.minimum(f, 0.0) - jnp.log1p(jnp.exp(-jnp.abs(f)))
        ls = jnp.where(lane1 < BIAS_LANES, ls, 0.0)
        dst = lax.broadcasted_iota(jnp.int32, (TM, TM), 0)
        src = lax.broadcasted_iota(jnp.int32, (TM, TM), 1)
        tri = jnp.where(src <= dst, 1.0, 0.0).astype(BF16)
        cs3 = _dot(tri, jnp.concatenate(_split3(ls), axis=1))
        carry = jnp.where(first, 0.0, carry_scr[0:1, :])
        c = cs3[:, 0:LANES] + cs3[:, LANES:2 * LANES] + cs3[:, 2 * LANES:3 * LANES] + carry
        carry_scr[...] = jnp.broadcast_to(c[TM - 1:TM, :], carry_scr.shape)
        hi, mid, lo = _split3(-LOG2_E * c)
        is_hi = functools.reduce(jnp.logical_or, [lane1 == PIECES * h for h in range(N_HEADS)])
        is_mid = functools.reduce(jnp.logical_or, [lane1 == PIECES * h + 1 for h in range(N_HEADS)])
        cb_ref[...] = jnp.where(is_hi, hi, jnp.where(is_mid, mid, lo))

    @pl.when(i == n_tiles)
    def _():
        _conv_module(xs_scr, ys_scr, cw_ref, cbias_ref, gc_ref, yc_ref)


def _ffn_proj_call(h, g_ff, w_in, w_out, g_mix, w_qvt, w_k, w_c, w_f, b_f, conv_w, conv_b, g_conv, layer):
    m = h.shape[0]
    n_tiles = m // TM
    per_seq = SEQ // TM
    cur = lambda i: jnp.minimum(i, n_tiles - 1)
    return pl.pallas_call(
        functools.partial(_ffn_proj_body, n_tiles=n_tiles),
        grid=(n_tiles + 1,),
        in_specs=[
            pl.BlockSpec((TM, D_MODEL), lambda i: (cur(i), 0)),
            _layer_block((1, D_MODEL), layer),
            _layer_block((D_MODEL, 2 * D_FF), layer),
            _layer_block((D_FF, D_MODEL), layer),
            _layer_block((1, D_MODEL), layer),
            _layer_block((2 * ATTN_WIDTH, D_MODEL), layer),
            _layer_block((D_MODEL, ATTN_WIDTH), layer),
            _layer_block((D_MODEL, 2 * CONV_CH), layer),
            _layer_block((D_MODEL, LANES), layer),
            _layer_block((1, LANES), layer),
            _layer_block((CONV_K, 8, CONV_CH), layer),
            _layer_block((1, CONV_CH), layer),
            _layer_block((1, CONV_CH), layer),
        ],
        out_specs=[
            pl.BlockSpec((TM, D_MODEL), lambda i: (cur(i), 0)),
            pl.BlockSpec((None, 2 * ATTN_WIDTH, TM), lambda i: (cur(i) // per_seq, 0, cur(i) % per_seq)),
            pl.BlockSpec((None, TM, ATTN_WIDTH), lambda i: (cur(i) // per_seq, cur(i) % per_seq, 0)),
            pl.BlockSpec((None, TM, LANES), lambda i: (cur(i) // per_seq, cur(i) % per_seq, 0)),
            pl.BlockSpec((TM, CONV_CH), lambda i: (jnp.maximum(i - 1, 0), 0)),
        ],
        out_shape=[
            jax.ShapeDtypeStruct((m, D_MODEL), F32),
            jax.ShapeDtypeStruct((BATCH, 2 * ATTN_WIDTH, SEQ), BF16),
            jax.ShapeDtypeStruct((BATCH, SEQ, ATTN_WIDTH), BF16),
            jax.ShapeDtypeStruct((BATCH, SEQ, LANES), BF16),
            jax.ShapeDtypeStruct((m, CONV_CH), BF16),
        ],
        scratch_shapes=[pltpu.VMEM((TM, D_MODEL), BF16),
                        pltpu.VMEM((TM, D_FF), BF16),
                        pltpu.VMEM((TM, D_MODEL), BF16),
                        pltpu.VMEM((CONV_CH // LANES, (TM // GROUP) * XS_PITCH, LANES), F32),
                        pltpu.VMEM((CONV_CH // LANES, (TM // GROUP) * YS_PITCH, LANES), F32),
                        pltpu.VMEM((HALO, CONV_CH), F32),
                        pltpu.VMEM((8, LANES), F32)],
        compiler_params=pltpu.CompilerParams(dimension_semantics=("arbitrary",),
                                             vmem_limit_bytes=VMEM_LIMIT),
        name="ffn_proj",
    )(h, g_ff, w_in, w_out, g_mix, w_qvt, w_k, w_c, w_f, b_f, conv_w, conv_b, g_conv)


def _attn_body(qt_ref, k_ref, cb_ref, vt_ref, o_ref, qa_scr, s_scr, m_scr, acc_scr):
    i = pl.program_id(1)
    nq = 2 * TQ
    sub = lax.broadcasted_iota(jnp.int32, (LANES, TQ), 0)

    @pl.when((pl.program_id(0) == 0) & (i == 0))
    def _():
        srow = lax.broadcasted_iota(jnp.int32, (LANES, nq), 0)
        scol = lax.broadcasted_iota(jnp.int32, (LANES, nq), 1)
        for g in range(N_PAIRS):
            first = PIECES * jnp.where(scol < TQ, 2 * g, 2 * g + 1)
            sel = (srow >= first) & (srow < first + PIECES)
            qa_scr[g, LANES:2 * LANES, :] = jnp.where(sel, 1.0, 0.0).astype(BF16)

    for g in range(N_PAIRS):
        qt = qt_ref[g * LANES:(g + 1) * LANES, :]
        zero = jnp.zeros_like(qt)
        qa_scr[g, 0:LANES, :] = jnp.concatenate(
            [jnp.where(sub < HEAD_DIM, qt, zero), jnp.where(sub >= HEAD_DIM, qt, zero)], axis=1)
    m_scr[...] = jnp.full(m_scr.shape, MASKED, F32)
    acc_scr[...] = jnp.zeros_like(acc_scr)

    ones = jnp.ones((ONES_ROWS, TK), BF16)

    def scores(j, g):
        k0 = pl.multiple_of(j * TK, TK)
        ka = jnp.concatenate([k_ref[pl.ds(k0, TK), g * LANES:(g + 1) * LANES],
                              cb_ref[pl.ds(k0, TK), :]], axis=1)
        s_scr[g] = _dot(ka, qa_scr[g])

    def softmax_pv(j, g, diagonal):
        k0 = pl.multiple_of(j * TK, TK)
        st = s_scr[g]
        if diagonal:
            key = lax.broadcasted_iota(jnp.int32, (TK, nq), 0)
            qcol = lax.broadcasted_iota(jnp.int32, (TK, nq), 1)
            st = jnp.where(key <= jnp.where(qcol >= TQ, qcol - TQ, qcol), st, MASKED)
        m_old = m_scr[g]
        m_new = jnp.maximum(m_old, jnp.max(st, axis=0, keepdims=True))
        pt = jnp.exp2(st - m_new).astype(BF16)
        va = jnp.concatenate([vt_ref[g * LANES:(g + 1) * LANES, pl.ds(k0, TK)], ones], axis=0)
        acc_scr[g] = jnp.exp2(m_old - m_new) * acc_scr[g] + _dot(va, pt)
        m_scr[g] = m_new

    for g in range(N_PAIRS):
        scores(0, g)

    def body(j, carry):
        for g in range(N_PAIRS):
            softmax_pv(j, g, False)
            scores(j + 1, g)
        return carry

    lax.fori_loop(0, i, body, 0)
    for g in range(N_PAIRS):
        softmax_pv(i, g, True)

    for g in range(N_PAIRS):
        acc = acc_scr[g]
        l = acc[LANES:LANES + 1, :]
        out_t = jnp.concatenate([acc[0:HEAD_DIM, 0:TQ] / l[:, 0:TQ],
                                 acc[HEAD_DIM:LANES, TQ:nq] / l[:, TQ:nq]], axis=0)
        o_ref[:, g * LANES:(g + 1) * LANES] = out_t.T.astype(BF16)


def _attn_call(qv_t, k, c_b):
    assert TQ == TK
    whole = lambda b, i: (b, 0, 0)
    return pl.pallas_call(
        _attn_body,
        grid=(BATCH, SEQ // TQ),
        in_specs=[
            pl.BlockSpec((None, ATTN_WIDTH, TQ), lambda b, i: (b, 0, i)),
            pl.BlockSpec((None, SEQ, ATTN_WIDTH), whole),
            pl.BlockSpec((None, SEQ, LANES), whole),
            pl.BlockSpec((None, ATTN_WIDTH, SEQ), lambda b, i: (b, 1, 0)),
        ],
        out_specs=pl.BlockSpec((None, TQ, ATTN_WIDTH), lambda b, i: (b, i, 0)),
        out_shape=jax.ShapeDtypeStruct((BATCH, SEQ, ATTN_WIDTH), BF16),
        scratch_shapes=[pltpu.VMEM((N_PAIRS, 2 * LANES, 2 * TQ), BF16),
                        pltpu.VMEM((N_PAIRS, TK, 2 * TQ), F32),
                        pltpu.VMEM((N_PAIRS, 1, 2 * TQ), F32),
                        pltpu.VMEM((N_PAIRS, LANES + ONES_ROWS, 2 * TQ), F32)],
        compiler_params=pltpu.CompilerParams(dimension_semantics=("arbitrary", "arbitrary"),
                                             vmem_limit_bytes=VMEM_LIMIT),
        name="attn",
    )(qv_t, k, c_b, qv_t)


def _mix_body(h_ref, g_ref, oa_ref, yc_ref, wg_ref, wao_ref, wco_ref, wout_ref, o_ref, m_scr):
    x = h_ref[...]
    u = _rms(x, g_ref[...]).astype(BF16)
    oa = oa_ref[...]
    yc = yc_ref[...]
    for c in range(D_MODEL // MIX_CHUNK):
        lo = c * MIX_CHUNK
        ga = _sigmoid(_dot(u, wg_ref[:, lo:lo + MIX_CHUNK]))
        gc = _sigmoid(_dot(u, wg_ref[:, D_MODEL + lo:D_MODEL + lo + MIX_CHUNK]))
        ya = _dot(oa, wao_ref[:, lo:lo + MIX_CHUNK])
        yv = _dot(yc, wco_ref[:, lo:lo + MIX_CHUNK])
        m_scr[:, lo:lo + MIX_CHUNK] = (ga * ya + gc * yv).astype(BF16)
    o_ref[...] = x + _dot(m_scr[...], wout_ref[...])


def _mix_call(h, g, o_attn, y_conv, w_g, w_ao, w_co, w_out, layer):
    m = h.shape[0]
    row = lambda i: (i, 0)
    return pl.pallas_call(
        _mix_body,
        grid=(m // TM,),
        in_specs=[
            pl.BlockSpec((TM, D_MODEL), row),
            _layer_block((1, D_MODEL), layer),
            pl.BlockSpec((TM, ATTN_WIDTH), row),
            pl.BlockSpec((TM, CONV_CH), row),
            _layer_block((D_MODEL, 2 * D_MODEL), layer),
            _layer_block((ATTN_WIDTH, D_MODEL), layer),
            _layer_block((CONV_CH, D_MODEL), layer),
            _layer_block((D_MODEL, D_MODEL), layer),
        ],
        out_specs=pl.BlockSpec((TM, D_MODEL), row),
        out_shape=jax.ShapeDtypeStruct((m, D_MODEL), F32),
        scratch_shapes=[pltpu.VMEM((TM, D_MODEL), BF16)],
        compiler_params=pltpu.CompilerParams(dimension_semantics=("arbitrary",),
                                             vmem_limit_bytes=VMEM_LIMIT),
        name="mixout",
    )(h, g, o_attn, y_conv, w_g, w_ao, w_co, w_out)


def kernel(x, p, g_ff1, w_ff1_in, w_ff1_out, g_mix, w_in, b_f, w_attn_out, conv_w, conv_b, g_conv,
           w_conv_out, w_out, g_ff2, w_ff2_in, w_ff2_out, g_ple, w_ple_gate, w_ple_proj, g_final):
    assert x.shape == (BATCH, SEQ, D_MODEL) and p.shape == (DEPTH, BATCH, SEQ, D_PLE)
    m = BATCH * SEQ
    row3 = lambda a: a[:, None, :]

    w_ff1_in_b, w_ff1_out_b = w_ff1_in.astype(BF16), w_ff1_out.astype(BF16)
    w_ff2_in_b, w_ff2_out_b = w_ff2_in.astype(BF16), w_ff2_out.astype(BF16)
    w_qvt = jnp.swapaxes(jnp.concatenate([w_in[:, :, 0:ATTN_WIDTH], w_in[:, :, 2 * ATTN_WIDTH:F0]], axis=2),
                         1, 2).astype(BF16)
    w_k = w_in[:, :, ATTN_WIDTH:2 * ATTN_WIDTH].astype(BF16)
    rep = jnp.pad(jnp.repeat(w_in[:, :, F0:C0], PIECES, axis=2),
                  ((0, 0), (0, 0), (0, LANES - BIAS_LANES))).astype(BF16)
    b_rep = jnp.pad(jnp.repeat(b_f, PIECES, axis=1), ((0, 0), (0, LANES - BIAS_LANES)))[:, None, :]
    w_c = w_in[:, :, C0:GA0].astype(BF16)
    w_g = w_in[:, :, GA0:].astype(BF16)
    w_ao, w_co, w_o = w_attn_out.astype(BF16), w_conv_out.astype(BF16), w_out.astype(BF16)
    w_pg, w_pp = w_ple_gate.astype(BF16), w_ple_proj.astype(BF16)
    conv_wp = jnp.broadcast_to(conv_w[:, :, None, :], (DEPTH, CONV_K, 8, CONV_CH))
    p2 = p.reshape(DEPTH, m, D_PLE)
    g_ff1_r, g_mix_r, g_ff2_r, g_ple_r = row3(g_ff1), row3(g_mix), row3(g_ff2), row3(g_ple)
    conv_b_r, g_conv_r = row3(conv_b), row3(g_conv)
    g_final_r = g_final[None, :]

    h = x.reshape(m, D_MODEL)
    for i in range(DEPTH):
        h, qv_t, k, c_b, y_conv = _ffn_proj_call(h, g_ff1_r, w_ff1_in_b, w_ff1_out_b, g_mix_r,
                                                 w_qvt, w_k, w_c, rep, b_rep, conv_wp, conv_b_r, g_conv_r, i)
        o_attn = _attn_call(qv_t, k, c_b)
        h = _mix_call(h, g_mix_r, o_attn.reshape(m, ATTN_WIDTH), y_conv, w_g, w_ao, w_co, w_o, i)
        h = _ffn_ple_call(h, g_ff2_r, w_ff2_in_b, w_ff2_out_b, p2, g_ple_r, w_pg, w_pp, i,
                          g_final=g_final_r if i == DEPTH - 1 else None)
    return h.reshape(BATCH, SEQ, D_MODEL)
```

```python
import functools

import jax
import jax.numpy as jnp
from jax import lax
from jax.experimental import pallas as pl
from jax.experimental.pallas import tpu as pltpu

D_MODEL = 1024
BATCH = 8
SEQ = 2048
DEPTH = 4
N_HEADS = 8
HEAD_DIM = 64
ATTN_WIDTH = N_HEADS * HEAD_DIM
CONV_CH = 512
CONV_K = 31
D_FF = 2816
D_PLE = 256
EPS = 1e-6
FFN_RES = 0.5

F0 = 3 * ATTN_WIDTH
C0 = F0 + N_HEADS
GA0 = C0 + 2 * CONV_CH

LANES = 128
N_PAIRS = N_HEADS // 2
PIECES = 3
BIAS_LANES = PIECES * N_HEADS
ONES_ROWS = 16
HALO = 32
TM = 512
FFN_CHUNK = 256
MIX_CHUNK = 256
TQ = 256
TK = 256
GROUP = TM // 8
XS_PITCH = HALO + GROUP + 4
YS_PITCH = GROUP + 4
CONV_ROWS = 16
MASKED = -1e30
LOG2_E = 1.4426950408889634
VMEM_LIMIT = 56 * 1024 * 1024

BF16 = jnp.bfloat16
F32 = jnp.float32


def _dot(a, b):
    return jnp.dot(a, b, preferred_element_type=F32)


def _dot_nt(a, b):
    return lax.dot_general(a, b, (((1,), (1,)), ((), ())), preferred_element_type=F32)


def _rms(x, g):
    ms = jnp.mean(x * x, axis=-1, keepdims=True)
    return x * lax.rsqrt(ms + EPS) * g


def _sigmoid(x):
    return 1.0 / (1.0 + jnp.exp(-x))


def _split3(x):
    hi = x.astype(BF16)
    r1 = x - hi.astype(F32)
    mid = r1.astype(BF16)
    lo = (r1 - mid.astype(F32)).astype(BF16)
    return hi, mid, lo


def _resident(shape):
    nd = len(shape)
    return pl.BlockSpec(shape, lambda *_: (0,) * nd, pipeline_mode=pl.Buffered(1))


def _layer_block(shape, layer):
    nd = len(shape)
    return pl.BlockSpec((None,) + tuple(shape), lambda *_: (layer,) + (0,) * nd,
                        pipeline_mode=pl.Buffered(1))


def _exact_zero(t):
    return jnp.where((t != t) & (t == t), 1.0, 0.0)


def _swiglu_half_step(x, g_ref, win_ref, wout_ref, u_scr, act_scr, side_work=None):
    u_scr[...] = _rms(x, g_ref[...]).astype(BF16)
    gates = []
    for c in range(D_FF // FFN_CHUNK):
        lo = c * FFN_CHUNK
        zero = side_work(c, gates[c - 2]) if side_work is not None and c >= 2 else None
        if zero is not None:
            zero = jnp.concatenate([zero, zero], axis=0)
            corner = u_scr[0:zero.shape[0], 0:zero.shape[1]]
            u_scr[0:zero.shape[0], 0:zero.shape[1]] = (corner.astype(F32) + zero).astype(BF16)
        u = u_scr[...]
        a = _dot(u, win_ref[:, lo:lo + FFN_CHUNK])
        b = _dot(u, win_ref[:, D_FF + lo:D_FF + lo + FFN_CHUNK])
        act = a * _sigmoid(a) * b
        gates.append(_exact_zero(act[0:8, 0:LANES]))
        act_scr[:, lo:lo + FFN_CHUNK] = act.astype(BF16)
    return x + FFN_RES * _dot(act_scr[...], wout_ref[...])


def _ffn_ple_body(*refs, with_final):
    h_ref, g_ref, win_ref, wout_ref, p_ref, gple_ref, wpg_ref, wpp_ref = refs[:8]
    pos = 8
    if with_final:
        gfin_ref = refs[pos]
        pos += 1
    o_ref, u_scr, act_scr = refs[pos:pos + 3]

    h1 = _swiglu_half_step(h_ref[...], g_ref, win_ref, wout_ref, u_scr, act_scr)
    u2 = _rms(h1, gple_ref[...]).astype(BF16)
    gate = _sigmoid(_dot(u2, wpg_ref[...]))
    h1 = h1 + gate * _dot(p_ref[...].astype(BF16), wpp_ref[...])
    if with_final:
        h1 = _rms(h1, gfin_ref[...])
    o_ref[...] = h1


def _ffn_ple_call(h, g, w_in, w_out, p, g_ple, w_pg, w_pp, layer, g_final=None):
    m = h.shape[0]
    row = lambda i: (i, 0)
    in_specs = [
        pl.BlockSpec((TM, D_MODEL), row),
        _layer_block((1, D_MODEL), layer),
        _layer_block((D_MODEL, 2 * D_FF), layer),
        _layer_block((D_FF, D_MODEL), layer),
        pl.BlockSpec((None, TM, D_PLE), lambda i: (layer, i, 0)),
        _layer_block((1, D_MODEL), layer),
        _layer_block((D_MODEL, D_MODEL), layer),
        _layer_block((D_PLE, D_MODEL), layer),
    ]
    args = [h, g, w_in, w_out, p, g_ple, w_pg, w_pp]
    if g_final is not None:
        in_specs.append(_resident((1, D_MODEL)))
        args.append(g_final)
    return pl.pallas_call(
        functools.partial(_ffn_ple_body, with_final=g_final is not None),
        grid=(m // TM,),
        in_specs=in_specs,
        out_specs=pl.BlockSpec((TM, D_MODEL), row),
        out_shape=jax.ShapeDtypeStruct((m, D_MODEL), F32),
        scratch_shapes=[pltpu.VMEM((TM, D_MODEL), BF16), pltpu.VMEM((TM, D_FF), BF16)],
        compiler_params=pltpu.CompilerParams(dimension_semantics=("arbitrary",),
                                             vmem_limit_bytes=VMEM_LIMIT),
        name="ffn_ple",
    )(*args)


def _stage_conv_input(av, tail_prev, xs_scr):
    for s in range(CONV_CH // LANES):
        cols = slice(s * LANES, (s + 1) * LANES)
        for b in range(TM // GROUP):
            base = b * XS_PITCH
            hist = tail_prev[:, cols] if b == 0 else av[b * GROUP - HALO:b * GROUP, cols]
            xs_scr[s, base:base + HALO, :] = hist
            xs_scr[s, base + HALO:base + HALO + GROUP, :] = av[b * GROUP:(b + 1) * GROUP, cols]


CONV_PIECES = (CONV_CH // LANES) * (GROUP // CONV_ROWS)


def _conv_piece(piece, xs_scr, ys_scr, cw_ref, cb_ref, gate=None):
    n_groups = TM // GROUP
    s, a0 = divmod(piece, GROUP // CONV_ROWS)
    a0 *= CONV_ROWS
    cols = slice(s * LANES, (s + 1) * LANES)
    if gate is not None:
        xs_scr[s, HALO + a0:HALO + a0 + 8, :] = xs_scr[s, HALO + a0:HALO + a0 + 8, :] + gate
    bias = jnp.broadcast_to(cb_ref[:, cols], (n_groups, LANES))
    accs = [bias] * CONV_ROWS
    for d in range(CONV_K):
        w = cw_ref[CONV_K - 1 - d, :, cols]
        for j in range(CONV_ROWS):
            x = xs_scr[s, pl.ds(HALO + a0 + j - d, n_groups, stride=XS_PITCH), :]
            accs[j] = accs[j] + w * x
    for j in range(CONV_ROWS):
        ys_scr[s, pl.ds(a0 + j, n_groups, stride=YS_PITCH), :] = accs[j]
    return _exact_zero(functools.reduce(jnp.add, accs))


def _conv_norm(ys_scr, gc_ref, yc_ref):
    for b in range(TM // GROUP):
        y = jnp.concatenate([ys_scr[s, b * YS_PITCH:b * YS_PITCH + GROUP, :]
                             for s in range(CONV_CH // LANES)], axis=1)
        y = _rms(y, gc_ref[...])
        yc_ref[b * GROUP:(b + 1) * GROUP, :] = (y * _sigmoid(y)).astype(BF16)


def _conv_module(xs_scr, ys_scr, cw_ref, cb_ref, gc_ref, yc_ref):
    for piece in range(CONV_PIECES):
        _conv_piece(piece, xs_scr, ys_scr, cw_ref, cb_ref)
    _conv_norm(ys_scr, gc_ref, yc_ref)


def _ffn_proj_body(h_ref, gff_ref, win_ref, wout_ref, gmix_ref, wqvt_ref, wk_ref, wc_ref, wf_ref, bf_ref,
                   cw_ref, cbias_ref, gc_ref,
                   h1_ref, qvt_ref, k_ref, cb_ref, yc_ref,
                   u_scr, act_scr, u2_scr, xs_scr, ys_scr, tail_scr, carry_scr, *, n_tiles):
    i = pl.program_id(0)

    @pl.when(i == 0)
    def _():
        xs_scr[...] = jnp.zeros_like(xs_scr)
        ys_scr[...] = jnp.zeros_like(ys_scr)
        tail_scr[...] = jnp.zeros_like(tail_scr)
        carry_scr[...] = jnp.zeros_like(carry_scr)

    @pl.when(i < n_tiles)
    def _():
        n_chunks = D_FF // FFN_CHUNK
        per_chunk = -(-CONV_PIECES // (n_chunks - 3))

        def conv_side_work(c, gate):
            pieces = range((c - 2) * per_chunk, min((c - 1) * per_chunk, CONV_PIECES))
            zeros = [_conv_piece(p, xs_scr, ys_scr, cw_ref, cbias_ref, gate) for p in pieces]
            return functools.reduce(jnp.add, zeros) if zeros else None

        h1 = _swiglu_half_step(h_ref[...], gff_ref, win_ref, wout_ref, u_scr, act_scr, conv_side_work)
        _conv_norm(ys_scr, gc_ref, yc_ref)
        h1_ref[...] = h1

        first = (i % (SEQ // TM)) == 0
        u2_scr[...] = _rms(h1, gmix_ref[...]).astype(BF16)
        glu_a = _dot(u2_scr[...], wc_ref[:, 0:CONV_CH])
        glu_b = _dot(u2_scr[...], wc_ref[:, CONV_CH:2 * CONV_CH])
        conv_in = glu_a * _sigmoid(glu_b)
        _stage_conv_input(conv_in, jnp.where(first, 0.0, tail_scr[...]), xs_scr)
        tail_scr[...] = conv_in[TM - HALO:TM, :]

        scale = LOG2_E / (HEAD_DIM ** 0.5)
        qvt_ref[0:ATTN_WIDTH, :] = (_dot_nt(wqvt_ref[0:ATTN_WIDTH, :], u2_scr[...]) * scale).astype(BF16)
        qvt_ref[ATTN_WIDTH:2 * ATTN_WIDTH, :] = _dot_nt(wqvt_ref[ATTN_WIDTH:2 * ATTN_WIDTH, :],
                                                        u2_scr[...]).astype(BF16)
        k_ref[...] = _dot(u2_scr[...], wk_ref[...]).astype(BF16)

        lane1 = lax.broadcasted_iota(jnp.int32, (1, LANES), 1)
        f = _dot(u2_scr[...], wf_ref[...]) + bf_ref[...]
        ls = jnp.minimum(f, 0.0) - jnp.log1p(jnp.exp(-jnp.abs(f)))
        ls = jnp.where(lane1 < BIAS_LANES, ls, 0.0)
        dst = lax.broadcasted_iota(jnp.int32, (TM, TM), 0)
        src = lax.broadcasted_iota(jnp.int32, (TM, TM), 1)
        tri = jnp.where(src <= dst, 1.0, 0.0).astype(BF16)
        cs3 = _dot(tri, jnp.concatenate(_split3(ls), axis=1))
        carry = jnp.where(first, 0.0, carry_scr[0:1, :])
        c = cs3[:, 0:LANES] + cs3[:, LANES:2 * LANES] + cs3[:, 2 * LANES:3 * LANES] + carry
        carry_scr[...] = jnp.broadcast_to(c[TM - 1:TM, :], carry_scr.shape)
        hi, mid, lo = _split3(-LOG2_E * c)
        is_hi = functools.reduce(jnp.logical_or, [lane1 == PIECES * h for h in range(N_HEADS)])
        is_mid = functools.reduce(jnp.logical_or, [lane1 == PIECES * h + 1 for h in range(N_HEADS)])
        cb_ref[...] = jnp.where(is_hi, hi, jnp.where(is_mid, mid, lo))

    @pl.when(i == n_tiles)
    def _():
        _conv_module(xs_scr, ys_scr, cw_ref, cbias_ref, gc_ref, yc_ref)


def _ffn_proj_call(h, g_ff, w_in, w_out, g_mix, w_qvt, w_k, w_c, w_f, b_f, conv_w, conv_b, g_conv, layer):
    m = h.shape[0]
    n_tiles = m // TM
    per_seq = SEQ // TM
    cur = lambda i: jnp.minimum(i, n_tiles - 1)
    return pl.pallas_call(
        functools.partial(_ffn_proj_body, n_tiles=n_tiles),
        grid=(n_tiles + 1,),
        in_specs=[
            pl.BlockSpec((TM, D_MODEL), lambda i: (cur(i), 0)),
            _layer_block((1, D_MODEL), layer),
            _layer_block((D_MODEL, 2 * D_FF), layer),
            _layer_block((D_FF, D_MODEL), layer),
            _layer_block((1, D_MODEL), layer),
            _layer_block((2 * ATTN_WIDTH, D_MODEL), layer),
            _layer_block((D_MODEL, ATTN_WIDTH), layer),
            _layer_block((D_MODEL, 2 * CONV_CH), layer),
            _layer_block((D_MODEL, LANES), layer),
            _layer_block((1, LANES), layer),
            _layer_block((CONV_K, 8, CONV_CH), layer),
            _layer_block((1, CONV_CH), layer),
            _layer_block((1, CONV_CH), layer),
        ],
        out_specs=[
            pl.BlockSpec((TM, D_MODEL), lambda i: (cur(i), 0)),
            pl.BlockSpec((None, 2 * ATTN_WIDTH, TM), lambda i: (cur(i) // per_seq, 0, cur(i) % per_seq)),
            pl.BlockSpec((None, TM, ATTN_WIDTH), lambda i: (cur(i) // per_seq, cur(i) % per_seq, 0)),
            pl.BlockSpec((None, TM, LANES), lambda i: (cur(i) // per_seq, cur(i) % per_seq, 0)),
            pl.BlockSpec((TM, CONV_CH), lambda i: (jnp.maximum(i - 1, 0), 0)),
        ],
        out_shape=[
            jax.ShapeDtypeStruct((m, D_MODEL), F32),
            jax.ShapeDtypeStruct((BATCH, 2 * ATTN_WIDTH, SEQ), BF16),
            jax.ShapeDtypeStruct((BATCH, SEQ, ATTN_WIDTH), BF16),
            jax.ShapeDtypeStruct((BATCH, SEQ, LANES), BF16),
            jax.ShapeDtypeStruct((m, CONV_CH), BF16),
        ],
        scratch_shapes=[pltpu.VMEM((TM, D_MODEL), BF16),
                        pltpu.VMEM((TM, D_FF), BF16),
                        pltpu.VMEM((TM, D_MODEL), BF16),
                        pltpu.VMEM((CONV_CH // LANES, (TM // GROUP) * XS_PITCH, LANES), F32),
                        pltpu.VMEM((CONV_CH // LANES, (TM // GROUP) * YS_PITCH, LANES), F32),
                        pltpu.VMEM((HALO, CONV_CH), F32),
                        pltpu.VMEM((8, LANES), F32)],
        compiler_params=pltpu.CompilerParams(dimension_semantics=("arbitrary",),
                                             vmem_limit_bytes=VMEM_LIMIT),
        name="ffn_proj",
    )(h, g_ff, w_in, w_out, g_mix, w_qvt, w_k, w_c, w_f, b_f, conv_w, conv_b, g_conv)


def _attn_body(qt_ref, k_ref, cb_ref, vt_ref, o_ref, qa_scr, s_scr, m_scr, acc_scr):
    i = pl.program_id(1)
    nq = 2 * TQ
    sub = lax.broadcasted_iota(jnp.int32, (LANES, TQ), 0)

    @pl.when((pl.program_id(0) == 0) & (i == 0))
    def _():
        srow = lax.broadcasted_iota(jnp.int32, (LANES, nq), 0)
        scol = lax.broadcasted_iota(jnp.int32, (LANES, nq), 1)
        for g in range(N_PAIRS):
            first = PIECES * jnp.where(scol < TQ, 2 * g, 2 * g + 1)
            sel = (srow >= first) & (srow < first + PIECES)
            qa_scr[g, LANES:2 * LANES, :] = jnp.where(sel, 1.0, 0.0).astype(BF16)

    for g in range(N_PAIRS):
        qt = qt_ref[g * LANES:(g + 1) * LANES, :]
        zero = jnp.zeros_like(qt)
        qa_scr[g, 0:LANES, :] = jnp.concatenate(
            [jnp.where(sub < HEAD_DIM, qt, zero), jnp.where(sub >= HEAD_DIM, qt, zero)], axis=1)
    m_scr[...] = jnp.full(m_scr.shape, MASKED, F32)
    acc_scr[...] = jnp.zeros_like(acc_scr)

    ones = jnp.ones((ONES_ROWS, TK), BF16)

    def scores(j, g):
        k0 = pl.multiple_of(j * TK, TK)
        ka = jnp.concatenate([k_ref[pl.ds(k0, TK), g * LANES:(g + 1) * LANES],
                              cb_ref[pl.ds(k0, TK), :]], axis=1)
        s_scr[g] = _dot(ka, qa_scr[g])

    def softmax_pv(j, g, diagonal):
        k0 = pl.multiple_of(j * TK, TK)
        st = s_scr[g]
        if diagonal:
            key = lax.broadcasted_iota(jnp.int32, (TK, nq), 0)
            qcol = lax.broadcasted_iota(jnp.int32, (TK, nq), 1)
            st = jnp.where(key <= jnp.where(qcol >= TQ, qcol - TQ, qcol), st, MASKED)
        m_old = m_scr[g]
        m_new = jnp.maximum(m_old, jnp.max(st, axis=0, keepdims=True))
        pt = jnp.exp2(st - m_new).astype(BF16)
        va = jnp.concatenate([vt_ref[g * LANES:(g + 1) * LANES, pl.ds(k0, TK)], ones], axis=0)
        acc_scr[g] = jnp.exp2(m_old - m_new) * acc_scr[g] + _dot(va, pt)
        m_scr[g] = m_new

    for g in range(N_PAIRS):
        scores(0, g)

    def body(j, carry):
        for g in range(N_PAIRS):
            softmax_pv(j, g, False)
            scores(j + 1, g)
        return carry

    lax.fori_loop(0, i, body, 0)
    for g in range(N_PAIRS):
        softmax_pv(i, g, True)

    for g in range(N_PAIRS):
        acc = acc_scr[g]
        l = acc[LANES:LANES + 1, :]
        out_t = jnp.concatenate([acc[0:HEAD_DIM, 0:TQ] / l[:, 0:TQ],
                                 acc[HEAD_DIM:LANES, TQ:nq] / l[:, TQ:nq]], axis=0)
        o_ref[:, g * LANES:(g + 1) * LANES] = out_t.T.astype(BF16)


def _attn_call(qv_t, k, c_b):
    assert TQ == TK
    whole = lambda b, i: (b, 0, 0)
    return pl.pallas_call(
        _attn_body,
        grid=(BATCH, SEQ // TQ),
        in_specs=[
            pl.BlockSpec((None, ATTN_WIDTH, TQ), lambda b, i: (b, 0, i)),
            pl.BlockSpec((None, SEQ, ATTN_WIDTH), whole),
            pl.BlockSpec((None, SEQ, LANES), whole),
            pl.BlockSpec((None, ATTN_WIDTH, SEQ), lambda b, i: (b, 1, 0)),
        ],
        out_specs=pl.BlockSpec((None, TQ, ATTN_WIDTH), lambda b, i: (b, i, 0)),
        out_shape=jax.ShapeDtypeStruct((BATCH, SEQ, ATTN_WIDTH), BF16),
        scratch_shapes=[pltpu.VMEM((N_PAIRS, 2 * LANES, 2 * TQ), BF16),
                        pltpu.VMEM((N_PAIRS, TK, 2 * TQ), F32),
                        pltpu.VMEM((N_PAIRS, 1, 2 * TQ), F32),
                        pltpu.VMEM((N_PAIRS, LANES + ONES_ROWS, 2 * TQ), F32)],
        compiler_params=pltpu.CompilerParams(dimension_semantics=("arbitrary", "arbitrary"),
                                             vmem_limit_bytes=VMEM_LIMIT),
        name="attn",
    )(qv_t, k, c_b, qv_t)


def _mix_body(h_ref, g_ref, oa_ref, yc_ref, wg_ref, wao_ref, wco_ref, wout_ref, o_ref, m_scr):
    x = h_ref[...]
    u = _rms(x, g_ref[...]).astype(BF16)
    oa = oa_ref[...]
    yc = yc_ref[...]
    for c in range(D_MODEL // MIX_CHUNK):
        lo = c * MIX_CHUNK
        ga = _sigmoid(_dot(u, wg_ref[:, lo:lo + MIX_CHUNK]))
        gc = _sigmoid(_dot(u, wg_ref[:, D_MODEL + lo:D_MODEL + lo + MIX_CHUNK]))
        ya = _dot(oa, wao_ref[:, lo:lo + MIX_CHUNK])
        yv = _dot(yc, wco_ref[:, lo:lo + MIX_CHUNK])
        m_scr[:, lo:lo + MIX_CHUNK] = (ga * ya + gc * yv).astype(BF16)
    o_ref[...] = x + _dot(m_scr[...], wout_ref[...])


def _mix_call(h, g, o_attn, y_conv, w_g, w_ao, w_co, w_out, layer):
    m = h.shape[0]
    row = lambda i: (i, 0)
    return pl.pallas_call(
        _mix_body,
        grid=(m // TM,),
        in_specs=[
            pl.BlockSpec((TM, D_MODEL), row),
            _layer_block((1, D_MODEL), layer),
            pl.BlockSpec((TM, ATTN_WIDTH), row),
            pl.BlockSpec((TM, CONV_CH), row),
            _layer_block((D_MODEL, 2 * D_MODEL), layer),
            _layer_block((ATTN_WIDTH, D_MODEL), layer),
            _layer_block((CONV_CH, D_MODEL), layer),
            _layer_block((D_MODEL, D_MODEL), layer),
        ],
        out_specs=pl.BlockSpec((TM, D_MODEL), row),
        out_shape=jax.ShapeDtypeStruct((m, D_MODEL), F32),
        scratch_shapes=[pltpu.VMEM((TM, D_MODEL), BF16)],
        compiler_params=pltpu.CompilerParams(dimension_semantics=("arbitrary",),
                                             vmem_limit_bytes=VMEM_LIMIT),
        name="mixout",
    )(h, g, o_attn, y_conv, w_g, w_ao, w_co, w_out)


def kernel(x, p, g_ff1, w_ff1_in, w_ff1_out, g_mix, w_in, b_f, w_attn_out, conv_w, conv_b, g_conv,
           w_conv_out, w_out, g_ff2, w_ff2_in, w_ff2_out, g_ple, w_ple_gate, w_ple_proj, g_final):
    assert x.shape == (BATCH, SEQ, D_MODEL) and p.shape == (DEPTH, BATCH, SEQ, D_PLE)
    m = BATCH * SEQ
    row3 = lambda a: a[:, None, :]

    w_ff1_in_b, w_ff1_out_b = w_ff1_in.astype(BF16), w_ff1_out.astype(BF16)
    w_ff2_in_b, w_ff2_out_b = w_ff2_in.astype(BF16), w_ff2_out.astype(BF16)
    w_qvt = jnp.swapaxes(jnp.concatenate([w_in[:, :, 0:ATTN_WIDTH], w_in[:, :, 2 * ATTN_WIDTH:F0]], axis=2),
                         1, 2).astype(BF16)
    w_k = w_in[:, :, ATTN_WIDTH:2 * ATTN_WIDTH].astype(BF16)
    rep = jnp.pad(jnp.repeat(w_in[:, :, F0:C0], PIECES, axis=2),
                  ((0, 0), (0, 0), (0, LANES - BIAS_LANES))).astype(BF16)
    b_rep = jnp.pad(jnp.repeat(b_f, PIECES, axis=1), ((0, 0), (0, LANES - BIAS_LANES)))[:, None, :]
    w_c = w_in[:, :, C0:GA0].astype(BF16)
    w_g = w_in[:, :, GA0:].astype(BF16)
    w_ao, w_co, w_o = w_attn_out.astype(BF16), w_conv_out.astype(BF16), w_out.astype(BF16)
    w_pg, w_pp = w_ple_gate.astype(BF16), w_ple_proj.astype(BF16)
    conv_wp = jnp.broadcast_to(conv_w[:, :, None, :], (DEPTH, CONV_K, 8, CONV_CH))
    p2 = p.reshape(DEPTH, m, D_PLE)
    g_ff1_r, g_mix_r, g_ff2_r, g_ple_r = row3(g_ff1), row3(g_mix), row3(g_ff2), row3(g_ple)
    conv_b_r, g_conv_r = row3(conv_b), row3(g_conv)
    g_final_r = g_final[None, :]

    h = x.reshape(m, D_MODEL)
    for i in range(DEPTH):
        h, qv_t, k, c_b, y_conv = _ffn_proj_call(h, g_ff1_r, w_ff1_in_b, w_ff1_out_b, g_mix_r,
                                                 w_qvt, w_k, w_c, rep, b_rep, conv_wp, conv_b_r, g_conv_r, i)
        o_attn = _attn_call(qv_t, k, c_b)
        h = _mix_call(h, g_mix_r, o_attn.reshape(m, ATTN_WIDTH), y_conv, w_g, w_ao, w_co, w_o, i)
        h = _ffn_ple_call(h, g_ff2_r, w_ff2_in_b, w_ff2_out_b, p2, g_ple_r, w_pg, w_pp, i,
                          g_final=g_final_r if i == DEPTH - 1 else None)
    return h.reshape(BATCH, SEQ, D_MODEL)
```

```python
import functools

import jax
import jax.numpy as jnp
from jax import lax
from jax.experimental import pallas as pl
from jax.experimental.pallas import tpu as pltpu

D_MODEL = 1024
BATCH = 8
SEQ = 2048
DEPTH = 4
N_HEADS = 8
HEAD_DIM = 64
ATTN_WIDTH = N_HEADS * HEAD_DIM
CONV_CH = 512
CONV_K = 31
D_FF = 2816
D_PLE = 256
EPS = 1e-6
FFN_RES = 0.5

F0 = 3 * ATTN_WIDTH
C0 = F0 + N_HEADS
GA0 = C0 + 2 * CONV_CH

LANES = 128
N_PAIRS = N_HEADS // 2
PIECES = 3
BIAS_LANES = PIECES * N_HEADS
ONES_ROWS = 16
HALO = 32
TM = 512
FFN_CHUNK = 256
MIX_CHUNK = 256
TQ = 256
TK = 256
GROUP = TM // 8
XS_PITCH = HALO + GROUP + 4
YS_PITCH = GROUP + 4
CONV_ROWS = 16
MASKED = -1e30
LOG2_E = 1.4426950408889634
VMEM_LIMIT = 56 * 1024 * 1024

BF16 = jnp.bfloat16
F32 = jnp.float32


def _dot(a, b):
    return jnp.dot(a, b.astype(BF16), preferred_element_type=F32)


def _dot_nt(a, b):
    return lax.dot_general(a, b, (((1,), (1,)), ((), ())), preferred_element_type=F32)


def _rms(x, g):
    ms = jnp.mean(x * x, axis=-1, keepdims=True)
    return x * lax.rsqrt(ms + EPS) * g


def _sigmoid(x):
    return 1.0 / (1.0 + jnp.exp(-x))


def _split3(x):
    hi = x.astype(BF16)
    r1 = x - hi.astype(F32)
    mid = r1.astype(BF16)
    lo = (r1 - mid.astype(F32)).astype(BF16)
    return hi, mid, lo


def _resident(shape):
    nd = len(shape)
    return pl.BlockSpec(shape, lambda *_: (0,) * nd, pipeline_mode=pl.Buffered(1))


def _layer_block(shape, layer):
    nd = len(shape)
    return pl.BlockSpec((None,) + tuple(shape), lambda *_: (layer,) + (0,) * nd,
                        pipeline_mode=pl.Buffered(1))


def _swiglu_half_step(x, g_ref, win_ref, wout_ref, u_scr, act_scr):
    u_scr[...] = _rms(x, g_ref[...]).astype(BF16)
    for c in range(D_FF // FFN_CHUNK):
        lo = c * FFN_CHUNK
        u = u_scr[...]
        a = _dot(u, win_ref[:, lo:lo + FFN_CHUNK])
        b = _dot(u, win_ref[:, D_FF + lo:D_FF + lo + FFN_CHUNK])
        act_scr[:, lo:lo + FFN_CHUNK] = (a * _sigmoid(a) * b).astype(BF16)
    return x + FFN_RES * _dot(act_scr[...], wout_ref[...])


def _ffn_ple_body(*refs, with_final):
    h_ref, g_ref, win_ref, wout_ref, p_ref, gple_ref, wpg_ref, wpp_ref = refs[:8]
    pos = 8
    if with_final:
        gfin_ref = refs[pos]
        pos += 1
    o_ref, u_scr, act_scr = refs[pos:pos + 3]

    h1 = _swiglu_half_step(h_ref[...], g_ref, win_ref, wout_ref, u_scr, act_scr)
    u2 = _rms(h1, gple_ref[...]).astype(BF16)
    gate = _sigmoid(_dot(u2, wpg_ref[...]))
    h1 = h1 + gate * _dot(p_ref[...].astype(BF16), wpp_ref[...])
    if with_final:
        h1 = _rms(h1, gfin_ref[...])
    o_ref[...] = h1


def _ffn_ple_call(h, g, w_in, w_out, p, g_ple, w_pg, w_pp, layer, g_final=None):
    m = h.shape[0]
    row = lambda i: (i, 0)
    in_specs = [
        pl.BlockSpec((TM, D_MODEL), row),
        _layer_block((1, D_MODEL), layer),
        _layer_block((D_MODEL, 2 * D_FF), layer),
        _layer_block((D_FF, D_MODEL), layer),
        pl.BlockSpec((None, TM, D_PLE), lambda i: (layer, i, 0)),
        _layer_block((1, D_MODEL), layer),
        _layer_block((D_MODEL, D_MODEL), layer),
        _layer_block((D_PLE, D_MODEL), layer),
    ]
    args = [h, g, w_in, w_out, p, g_ple, w_pg, w_pp]
    if g_final is not None:
        in_specs.append(_resident((1, D_MODEL)))
        args.append(g_final)
    return pl.pallas_call(
        functools.partial(_ffn_ple_body, with_final=g_final is not None),
        grid=(m // TM,),
        in_specs=in_specs,
        out_specs=pl.BlockSpec((TM, D_MODEL), row),
        out_shape=jax.ShapeDtypeStruct((m, D_MODEL), F32),
        scratch_shapes=[pltpu.VMEM((TM, D_MODEL), BF16), pltpu.VMEM((TM, D_FF), BF16)],
        compiler_params=pltpu.CompilerParams(dimension_semantics=("arbitrary",),
                                             vmem_limit_bytes=VMEM_LIMIT),
        name="ffn_ple",
    )(*args)


def _stage_conv_input(av, tail_prev, xs_scr):
    for s in range(CONV_CH // LANES):
        cols = slice(s * LANES, (s + 1) * LANES)
        for b in range(TM // GROUP):
            base = b * XS_PITCH
            hist = tail_prev[:, cols] if b == 0 else av[b * GROUP - HALO:b * GROUP, cols]
            xs_scr[s, base:base + HALO, :] = hist
            xs_scr[s, base + HALO:base + HALO + GROUP, :] = av[b * GROUP:(b + 1) * GROUP, cols]


def _conv_module(xs_scr, ys_scr, cw_ref, cb_ref, gc_ref, yc_ref):
    n_groups = TM // GROUP
    for s in range(CONV_CH // LANES):
        cols = slice(s * LANES, (s + 1) * LANES)
        bias = jnp.broadcast_to(cb_ref[:, cols], (n_groups, LANES))
        for a0 in range(0, GROUP, CONV_ROWS):
            accs = [bias] * CONV_ROWS
            for d in range(CONV_K):
                w = cw_ref[CONV_K - 1 - d, :, cols]
                for j in range(CONV_ROWS):
                    x = xs_scr[s, pl.ds(HALO + a0 + j - d, n_groups, stride=XS_PITCH), :]
                    accs[j] = accs[j] + w * x
            for j in range(CONV_ROWS):
                ys_scr[s, pl.ds(a0 + j, n_groups, stride=YS_PITCH), :] = accs[j]
    for b in range(n_groups):
        y = jnp.concatenate([ys_scr[s, b * YS_PITCH:b * YS_PITCH + GROUP, :]
                             for s in range(CONV_CH // LANES)], axis=1)
        y = _rms(y, gc_ref[...])
        yc_ref[b * GROUP:(b + 1) * GROUP, :] = (y * _sigmoid(y)).astype(BF16)


def _ffn_proj_body(h_ref, gff_ref, win_ref, wout_ref, gmix_ref, wqvt_ref, wk_ref, wc_ref, wf_ref, bf_ref,
                   cw_ref, cbias_ref, gc_ref,
                   h1_ref, qvt_ref, k_ref, cb_ref, yc_ref,
                   u_scr, act_scr, u2_scr, xs_scr, ys_scr, tail_scr, carry_scr, *, n_tiles):
    i = pl.program_id(0)

    @pl.when(i == 0)
    def _():
        xs_scr[...] = jnp.zeros_like(xs_scr)
        ys_scr[...] = jnp.zeros_like(ys_scr)
        tail_scr[...] = jnp.zeros_like(tail_scr)
        carry_scr[...] = jnp.zeros_like(carry_scr)

    @pl.when(i < n_tiles)
    def _():
        _conv_module(xs_scr, ys_scr, cw_ref, cbias_ref, gc_ref, yc_ref)

        h1 = _swiglu_half_step(h_ref[...], gff_ref, win_ref, wout_ref, u_scr, act_scr)
        h1_ref[...] = h1

        first = (i % (SEQ // TM)) == 0
        u2_scr[...] = _rms(h1, gmix_ref[...]).astype(BF16)
        glu_a = _dot(u2_scr[...], wc_ref[:, 0:CONV_CH])
        glu_b = _dot(u2_scr[...], wc_ref[:, CONV_CH:2 * CONV_CH])
        conv_in = glu_a * _sigmoid(glu_b)
        _stage_conv_input(conv_in, jnp.where(first, 0.0, tail_scr[...]), xs_scr)
        tail_scr[...] = conv_in[TM - HALO:TM, :]

        scale = LOG2_E / (HEAD_DIM ** 0.5)
        qvt_ref[0:ATTN_WIDTH, :] = (_dot_nt(wqvt_ref[0:ATTN_WIDTH, :], u2_scr[...]) * scale).astype(BF16)
        qvt_ref[ATTN_WIDTH:2 * ATTN_WIDTH, :] = _dot_nt(wqvt_ref[ATTN_WIDTH:2 * ATTN_WIDTH, :],
                                                        u2_scr[...]).astype(BF16)
        k_ref[...] = _dot(u2_scr[...], wk_ref[...]).astype(BF16)

        lane1 = lax.broadcasted_iota(jnp.int32, (1, LANES), 1)
        f = _dot(u2_scr[...], wf_ref[...]) + bf_ref[...]
        ls = jnp.minimum(f, 0.0) - jnp.log1p(jnp.exp(-jnp.abs(f)))
        ls = jnp.where(lane1 < BIAS_LANES, ls, 0.0)
        dst = lax.broadcasted_iota(jnp.int32, (TM, TM), 0)
        src = lax.broadcasted_iota(jnp.int32, (TM, TM), 1)
        tri = jnp.where(src <= dst, 1.0, 0.0).astype(BF16)
        cs3 = _dot(tri, jnp.concatenate(_split3(ls), axis=1))
        carry = jnp.where(first, 0.0, carry_scr[0:1, :])
        c = cs3[:, 0:LANES] + cs3[:, LANES:2 * LANES] + cs3[:, 2 * LANES:3 * LANES] + carry
        carry_scr[...] = jnp.broadcast_to(c[TM - 1:TM, :], carry_scr.shape)
        hi, mid, lo = _split3(-LOG2_E * c)
        is_hi = functools.reduce(jnp.logical_or, [lane1 == PIECES * h for h in range(N_HEADS)])
        is_mid = functools.reduce(jnp.logical_or, [lane1 == PIECES * h + 1 for h in range(N_HEADS)])
        cb_ref[...] = jnp.where(is_hi, hi, jnp.where(is_mid, mid, lo))

    @pl.when(i == n_tiles)
    def _():
        _conv_module(xs_scr, ys_scr, cw_ref, cbias_ref, gc_ref, yc_ref)


def _ffn_proj_call(h, g_ff, w_in, w_out, g_mix, w_qvt, w_k, w_c, w_f, b_f, conv_w, conv_b, g_conv, layer):
    m = h.shape[0]
    n_tiles = m // TM
    per_seq = SEQ // TM
    cur = lambda i: jnp.minimum(i, n_tiles - 1)
    return pl.pallas_call(
        functools.partial(_ffn_proj_body, n_tiles=n_tiles),
        grid=(n_tiles + 1,),
        in_specs=[
            pl.BlockSpec((TM, D_MODEL), lambda i: (cur(i), 0)),
            _layer_block((1, D_MODEL), layer),
            _layer_block((D_MODEL, 2 * D_FF), layer),
            _layer_block((D_FF, D_MODEL), layer),
            _layer_block((1, D_MODEL), layer),
            _layer_block((2 * ATTN_WIDTH, D_MODEL), layer),
            _layer_block((D_MODEL, ATTN_WIDTH), layer),
            _layer_block((D_MODEL, 2 * CONV_CH), layer),
            _layer_block((D_MODEL, LANES), layer),
            _layer_block((1, LANES), layer),
            _layer_block((CONV_K, 8, CONV_CH), layer),
            _layer_block((1, CONV_CH), layer),
            _layer_block((1, CONV_CH), layer),
        ],
        out_specs=[
            pl.BlockSpec((TM, D_MODEL), lambda i: (cur(i), 0)),
            pl.BlockSpec((None, 2 * ATTN_WIDTH, TM), lambda i: (cur(i) // per_seq, 0, cur(i) % per_seq)),
            pl.BlockSpec((None, TM, ATTN_WIDTH), lambda i: (cur(i) // per_seq, cur(i) % per_seq, 0)),
            pl.BlockSpec((None, TM, LANES), lambda i: (cur(i) // per_seq, cur(i) % per_seq, 0)),
            pl.BlockSpec((TM, CONV_CH), lambda i: (jnp.maximum(i - 1, 0), 0)),
        ],
        out_shape=[
            jax.ShapeDtypeStruct((m, D_MODEL), F32),
            jax.ShapeDtypeStruct((BATCH, 2 * ATTN_WIDTH, SEQ), BF16),
            jax.ShapeDtypeStruct((BATCH, SEQ, ATTN_WIDTH), BF16),
            jax.ShapeDtypeStruct((BATCH, SEQ, LANES), BF16),
            jax.ShapeDtypeStruct((m, CONV_CH), BF16),
        ],
        scratch_shapes=[pltpu.VMEM((TM, D_MODEL), BF16),
                        pltpu.VMEM((TM, D_FF), BF16),
                        pltpu.VMEM((TM, D_MODEL), BF16),
                        pltpu.VMEM((CONV_CH // LANES, (TM // GROUP) * XS_PITCH, LANES), F32),
                        pltpu.VMEM((CONV_CH // LANES, (TM // GROUP) * YS_PITCH, LANES), F32),
                        pltpu.VMEM((HALO, CONV_CH), F32),
                        pltpu.VMEM((8, LANES), F32)],
        compiler_params=pltpu.CompilerParams(dimension_semantics=("arbitrary",),
                                             vmem_limit_bytes=VMEM_LIMIT),
        name="ffn_proj",
    )(h, g_ff, w_in, w_out, g_mix, w_qvt, w_k, w_c, w_f, b_f, conv_w, conv_b, g_conv)


def _attn_body(qt_ref, k_ref, cb_ref, vt_ref, o_ref, qa_scr, s_scr, m_scr, acc_scr):
    i = pl.program_id(1)
    nq = 2 * TQ
    sub = lax.broadcasted_iota(jnp.int32, (LANES, TQ), 0)

    @pl.when((pl.program_id(0) == 0) & (i == 0))
    def _():
        srow = lax.broadcasted_iota(jnp.int32, (LANES, nq), 0)
        scol = lax.broadcasted_iota(jnp.int32, (LANES, nq), 1)
        for g in range(N_PAIRS):
            first = PIECES * jnp.where(scol < TQ, 2 * g, 2 * g + 1)
            sel = (srow >= first) & (srow < first + PIECES)
            qa_scr[g, LANES:2 * LANES, :] = jnp.where(sel, 1.0, 0.0).astype(BF16)

    for g in range(N_PAIRS):
        qt = qt_ref[g * LANES:(g + 1) * LANES, :]
        zero = jnp.zeros_like(qt)
        qa_scr[g, 0:LANES, :] = jnp.concatenate(
            [jnp.where(sub < HEAD_DIM, qt, zero), jnp.where(sub >= HEAD_DIM, qt, zero)], axis=1)
    m_scr[...] = jnp.full(m_scr.shape, MASKED, F32)
    acc_scr[...] = jnp.zeros_like(acc_scr)

    ones = jnp.ones((ONES_ROWS, TK), BF16)

    def scores(j, g):
        k0 = pl.multiple_of(j * TK, TK)
        ka = jnp.concatenate([k_ref[pl.ds(k0, TK), g * LANES:(g + 1) * LANES],
                              cb_ref[pl.ds(k0, TK), :]], axis=1)
        s_scr[g] = _dot(ka, qa_scr[g])

    def softmax_pv(j, g, diagonal):
        k0 = pl.multiple_of(j * TK, TK)
        st = s_scr[g]
        if diagonal:
            key = lax.broadcasted_iota(jnp.int32, (TK, nq), 0)
            qcol = lax.broadcasted_iota(jnp.int32, (TK, nq), 1)
            st = jnp.where(key <= jnp.where(qcol >= TQ, qcol - TQ, qcol), st, MASKED)
        m_old = m_scr[g]
        m_new = jnp.maximum(m_old, jnp.max(st, axis=0, keepdims=True))
        pt = jnp.exp2(st - m_new).astype(BF16)
        va = jnp.concatenate([vt_ref[g * LANES:(g + 1) * LANES, pl.ds(k0, TK)], ones], axis=0)
        acc_scr[g] = jnp.exp2(m_old - m_new) * acc_scr[g] + _dot(va, pt)
        m_scr[g] = m_new

    for g in range(N_PAIRS):
        scores(0, g)

    def body(j, carry):
        for g in range(N_PAIRS):
            softmax_pv(j, g, False)
            scores(j + 1, g)
        return carry

    lax.fori_loop(0, i, body, 0)
    for g in range(N_PAIRS):
        softmax_pv(i, g, True)

    for g in range(N_PAIRS):
        acc = acc_scr[g]
        l = acc[LANES:LANES + 1, :]
        out_t = jnp.concatenate([acc[0:HEAD_DIM, 0:TQ] / l[:, 0:TQ],
                                 acc[HEAD_DIM:LANES, TQ:nq] / l[:, TQ:nq]], axis=0)
        o_ref[:, g * LANES:(g + 1) * LANES] = out_t.T.astype(BF16)


def _attn_call(qv_t, k, c_b):
    assert TQ == TK
    whole = lambda b, i: (b, 0, 0)
    return pl.pallas_call(
        _attn_body,
        grid=(BATCH, SEQ // TQ),
        in_specs=[
            pl.BlockSpec((None, ATTN_WIDTH, TQ), lambda b, i: (b, 0, i)),
            pl.BlockSpec((None, SEQ, ATTN_WIDTH), whole),
            pl.BlockSpec((None, SEQ, LANES), whole),
            pl.BlockSpec((None, ATTN_WIDTH, SEQ), lambda b, i: (b, 1, 0)),
        ],
        out_specs=pl.BlockSpec((None, TQ, ATTN_WIDTH), lambda b, i: (b, i, 0)),
        out_shape=jax.ShapeDtypeStruct((BATCH, SEQ, ATTN_WIDTH), BF16),
        scratch_shapes=[pltpu.VMEM((N_PAIRS, 2 * LANES, 2 * TQ), BF16),
                        pltpu.VMEM((N_PAIRS, TK, 2 * TQ), F32),
                        pltpu.VMEM((N_PAIRS, 1, 2 * TQ), F32),
                        pltpu.VMEM((N_PAIRS, LANES + ONES_ROWS, 2 * TQ), F32)],
        compiler_params=pltpu.CompilerParams(dimension_semantics=("arbitrary", "arbitrary"),
                                             vmem_limit_bytes=VMEM_LIMIT),
        name="attn",
    )(qv_t, k, c_b, qv_t)


def _mix_body(h_ref, g_ref, oa_ref, yc_ref, wg_ref, wao_ref, wco_ref, wout_ref, o_ref, m_scr):
    x = h_ref[...]
    u = _rms(x, g_ref[...]).astype(BF16)
    oa = oa_ref[...]
    yc = yc_ref[...]
    for c in range(D_MODEL // MIX_CHUNK):
        lo = c * MIX_CHUNK
        ga = _sigmoid(_dot(u, wg_ref[:, lo:lo + MIX_CHUNK]))
        gc = _sigmoid(_dot(u, wg_ref[:, D_MODEL + lo:D_MODEL + lo + MIX_CHUNK]))
        ya = _dot(oa, wao_ref[:, lo:lo + MIX_CHUNK])
        yv = _dot(yc, wco_ref[:, lo:lo + MIX_CHUNK])
        m_scr[:, lo:lo + MIX_CHUNK] = (ga * ya + gc * yv).astype(BF16)
    o_ref[...] = x + _dot(m_scr[...], wout_ref[...])


def _mix_call(h, g, o_attn, y_conv, w_g, w_ao, w_co, w_out, layer):
    m = h.shape[0]
    row = lambda i: (i, 0)
    return pl.pallas_call(
        _mix_body,
        grid=(m // TM,),
        in_specs=[
            pl.BlockSpec((TM, D_MODEL), row),
            _layer_block((1, D_MODEL), layer),
            pl.BlockSpec((TM, ATTN_WIDTH), row),
            pl.BlockSpec((TM, CONV_CH), row),
            _layer_block((D_MODEL, 2 * D_MODEL), layer),
            _layer_block((ATTN_WIDTH, D_MODEL), layer),
            _layer_block((CONV_CH, D_MODEL), layer),
            _layer_block((D_MODEL, D_MODEL), layer),
        ],
        out_specs=pl.BlockSpec((TM, D_MODEL), row),
        out_shape=jax.ShapeDtypeStruct((m, D_MODEL), F32),
        scratch_shapes=[pltpu.VMEM((TM, D_MODEL), BF16)],
        compiler_params=pltpu.CompilerParams(dimension_semantics=("arbitrary",),
                                             vmem_limit_bytes=VMEM_LIMIT),
        name="mixout",
    )(h, g, o_attn, y_conv, w_g, w_ao, w_co, w_out)


def kernel(x, p, g_ff1, w_ff1_in, w_ff1_out, g_mix, w_in, b_f, w_attn_out, conv_w, conv_b, g_conv,
           w_conv_out, w_out, g_ff2, w_ff2_in, w_ff2_out, g_ple, w_ple_gate, w_ple_proj, g_final):
    assert x.shape == (BATCH, SEQ, D_MODEL) and p.shape == (DEPTH, BATCH, SEQ, D_PLE)
    m = BATCH * SEQ
    row3 = lambda a: a[:, None, :]

    w_ff1_in_b = w_ff1_in.astype(BF16)
    w_qvt = jnp.swapaxes(jnp.concatenate([w_in[:, :, 0:ATTN_WIDTH], w_in[:, :, 2 * ATTN_WIDTH:F0]], axis=2),
                         1, 2).astype(BF16)
    w_k = w_in[:, :, ATTN_WIDTH:2 * ATTN_WIDTH].astype(BF16)
    rep = jnp.pad(jnp.repeat(w_in[:, :, F0:C0], PIECES, axis=2),
                  ((0, 0), (0, 0), (0, LANES - BIAS_LANES))).astype(BF16)
    b_rep = jnp.pad(jnp.repeat(b_f, PIECES, axis=1), ((0, 0), (0, LANES - BIAS_LANES)))[:, None, :]
    w_c = w_in[:, :, C0:GA0].astype(BF16)
    w_g = w_in[:, :, GA0:].astype(BF16)
    conv_wp = jnp.broadcast_to(conv_w[:, :, None, :], (DEPTH, CONV_K, 8, CONV_CH))
    p2 = p.reshape(DEPTH, m, D_PLE)
    g_ff1_r, g_mix_r, g_ff2_r, g_ple_r = row3(g_ff1), row3(g_mix), row3(g_ff2), row3(g_ple)
    conv_b_r, g_conv_r = row3(conv_b), row3(g_conv)
    g_final_r = g_final[None, :]

    h = x.reshape(m, D_MODEL)
    for i in range(DEPTH):
        h, qv_t, k, c_b, y_conv = _ffn_proj_call(h, g_ff1_r, w_ff1_in_b, w_ff1_out, g_mix_r,
                                                 w_qvt, w_k, w_c, rep, b_rep, conv_wp, conv_b_r, g_conv_r, i)
        o_attn = _attn_call(qv_t, k, c_b)
        h = _mix_call(h, g_mix_r, o_attn.reshape(m, ATTN_WIDTH), y_conv, w_g,
                      w_attn_out, w_conv_out, w_out, i)
        h = _ffn_ple_call(h, g_ff2_r, w_ff2_in, w_ff2_out, p2, g_ple_r, w_ple_gate, w_ple_proj, i,
                          g_final=g_final_r if i == DEPTH - 1 else None)
    return h.reshape(BATCH, SEQ, D_MODEL)
```

```python
import functools

import jax
import jax.numpy as jnp
from jax import lax
from jax.experimental import pallas as pl
from jax.experimental.pallas import tpu as pltpu

D_MODEL = 1024
BATCH = 8
SEQ = 2048
DEPTH = 4
N_HEADS = 8
HEAD_DIM = 64
ATTN_WIDTH = N_HEADS * HEAD_DIM
CONV_CH = 512
CONV_K = 31
D_FF = 2816
D_PLE = 256
EPS = 1e-6
FFN_RES = 0.5

F0 = 3 * ATTN_WIDTH
C0 = F0 + N_HEADS
GA0 = C0 + 2 * CONV_CH

LANES = 128
N_PAIRS = N_HEADS // 2
PIECES = 3
BIAS_LANES = PIECES * N_HEADS
ONES_ROWS = 16
HALO = 32
TM = 512
FFN_CHUNK = 256
MIX_CHUNK = 256
TQ = 512
TK = 512
GROUP = TM // 8
XS_PITCH = HALO + GROUP + 4
YS_PITCH = GROUP + 4
CONV_ROWS = 16
MASKED = -1e30
LOG2_E = 1.4426950408889634
VMEM_LIMIT = 56 * 1024 * 1024
VMEM_LIMIT_FFN_PROJ = 60 * 1024 * 1024

BF16 = jnp.bfloat16
F32 = jnp.float32


def _dot(a, b):
    return jnp.dot(a, b.astype(BF16), preferred_element_type=F32)


def _dot_nt(a, b):
    return lax.dot_general(a, b, (((1,), (1,)), ((), ())), preferred_element_type=F32)


def _rms(x, g):
    ms = jnp.mean(x * x, axis=-1, keepdims=True)
    return x * lax.rsqrt(ms + EPS) * g


def _sigmoid(x):
    return 1.0 / (1.0 + jnp.exp(-x))


def _split3(x):
    hi = x.astype(BF16)
    r1 = x - hi.astype(F32)
    mid = r1.astype(BF16)
    lo = (r1 - mid.astype(F32)).astype(BF16)
    return hi, mid, lo


def _resident(shape):
    nd = len(shape)
    return pl.BlockSpec(shape, lambda *_: (0,) * nd, pipeline_mode=pl.Buffered(1))


def _layer_block(shape, layer):
    nd = len(shape)
    return pl.BlockSpec((None,) + tuple(shape), lambda *_: (layer,) + (0,) * nd,
                        pipeline_mode=pl.Buffered(1))


def _swiglu_half_step(x, g_ref, win_ref, wout_ref, u_scr, act_scr):
    u_scr[...] = _rms(x, g_ref[...]).astype(BF16)
    for c in range(D_FF // FFN_CHUNK):
        lo = c * FFN_CHUNK
        u = u_scr[...]
        a = _dot(u, win_ref[:, lo:lo + FFN_CHUNK])
        b = _dot(u, win_ref[:, D_FF + lo:D_FF + lo + FFN_CHUNK])
        act_scr[:, lo:lo + FFN_CHUNK] = (a * _sigmoid(a) * b).astype(BF16)
    return x + FFN_RES * _dot(act_scr[...], wout_ref[...])


def _ffn_ple_body(*refs, with_final):
    h_ref, g_ref, win_ref, wout_ref, p_ref, gple_ref, wpg_ref, wpp_ref = refs[:8]
    pos = 8
    if with_final:
        gfin_ref = refs[pos]
        pos += 1
    o_ref, u_scr, act_scr = refs[pos:pos + 3]

    h1 = _swiglu_half_step(h_ref[...], g_ref, win_ref, wout_ref, u_scr, act_scr)
    u2 = _rms(h1, gple_ref[...]).astype(BF16)
    gate = _sigmoid(_dot(u2, wpg_ref[...]))
    h1 = h1 + gate * _dot(p_ref[...].astype(BF16), wpp_ref[...])
    if with_final:
        h1 = _rms(h1, gfin_ref[...])
    o_ref[...] = h1


def _ffn_ple_call(h, g, w_in, w_out, p, g_ple, w_pg, w_pp, layer, g_final=None):
    m = h.shape[0]
    row = lambda i: (i, 0)
    in_specs = [
        pl.BlockSpec((TM, D_MODEL), row),
        _layer_block((1, D_MODEL), layer),
        _layer_block((D_MODEL, 2 * D_FF), layer),
        _layer_block((D_FF, D_MODEL), layer),
        pl.BlockSpec((None, TM, D_PLE), lambda i: (layer, i, 0)),
        _layer_block((1, D_MODEL), layer),
        _layer_block((D_MODEL, D_MODEL), layer),
        _layer_block((D_PLE, D_MODEL), layer),
    ]
    args = [h, g, w_in, w_out, p, g_ple, w_pg, w_pp]
    if g_final is not None:
        in_specs.append(_resident((1, D_MODEL)))
        args.append(g_final)
    return pl.pallas_call(
        functools.partial(_ffn_ple_body, with_final=g_final is not None),
        grid=(m // TM,),
        in_specs=in_specs,
        out_specs=pl.BlockSpec((TM, D_MODEL), row),
        out_shape=jax.ShapeDtypeStruct((m, D_MODEL), F32),
        scratch_shapes=[pltpu.VMEM((TM, D_MODEL), BF16), pltpu.VMEM((TM, D_FF), BF16)],
        compiler_params=pltpu.CompilerParams(dimension_semantics=("arbitrary",),
                                             vmem_limit_bytes=VMEM_LIMIT),
        name="ffn_ple",
    )(*args)


def _stage_conv_input(av, tail_prev, xs_scr):
    for s in range(CONV_CH // LANES):
        cols = slice(s * LANES, (s + 1) * LANES)
        for b in range(TM // GROUP):
            base = b * XS_PITCH
            hist = tail_prev[:, cols] if b == 0 else av[b * GROUP - HALO:b * GROUP, cols]
            xs_scr[s, base:base + HALO, :] = hist
            xs_scr[s, base + HALO:base + HALO + GROUP, :] = av[b * GROUP:(b + 1) * GROUP, cols]


def _conv_module(xs_scr, ys_scr, cw_ref, cb_ref, gc_ref, yc_ref):
    n_groups = TM // GROUP
    for s in range(CONV_CH // LANES):
        cols = slice(s * LANES, (s + 1) * LANES)
        bias = jnp.broadcast_to(cb_ref[:, cols], (n_groups, LANES))
        for a0 in range(0, GROUP, CONV_ROWS):
            accs = [bias] * CONV_ROWS
            for d in range(CONV_K):
                w = cw_ref[CONV_K - 1 - d, :, cols]
                for j in range(CONV_ROWS):
                    x = xs_scr[s, pl.ds(HALO + a0 + j - d, n_groups, stride=XS_PITCH), :]
                    accs[j] = accs[j] + w * x
            for j in range(CONV_ROWS):
                ys_scr[s, pl.ds(a0 + j, n_groups, stride=YS_PITCH), :] = accs[j]
    for b in range(n_groups):
        y = jnp.concatenate([ys_scr[s, b * YS_PITCH:b * YS_PITCH + GROUP, :]
                             for s in range(CONV_CH // LANES)], axis=1)
        y = _rms(y, gc_ref[...])
        yc_ref[b * GROUP:(b + 1) * GROUP, :] = (y * _sigmoid(y)).astype(BF16)


def _ffn_proj_body(h_ref, gff_ref, win_ref, wout_ref, gmix_ref, wqvt_ref, wk_ref, wc_ref, wf_ref, bf_ref,
                   cw_ref, cbias_ref, gc_ref,
                   h1_ref, qvt_ref, k_ref, cb_ref, yc_ref,
                   u_scr, act_scr, u2_scr, xs_scr, ys_scr, tail_scr, carry_scr, *, n_tiles):
    i = pl.program_id(0)

    @pl.when(i == 0)
    def _():
        xs_scr[...] = jnp.zeros_like(xs_scr)
        ys_scr[...] = jnp.zeros_like(ys_scr)
        tail_scr[...] = jnp.zeros_like(tail_scr)
        carry_scr[...] = jnp.zeros_like(carry_scr)

    @pl.when(i < n_tiles)
    def _():
        _conv_module(xs_scr, ys_scr, cw_ref, cbias_ref, gc_ref, yc_ref)

        h1 = _swiglu_half_step(h_ref[...], gff_ref, win_ref, wout_ref, u_scr, act_scr)
        h1_ref[...] = h1

        first = (i % (SEQ // TM)) == 0
        u2_scr[...] = _rms(h1, gmix_ref[...]).astype(BF16)
        glu_a = _dot(u2_scr[...], wc_ref[:, 0:CONV_CH])
        glu_b = _dot(u2_scr[...], wc_ref[:, CONV_CH:2 * CONV_CH])
        conv_in = glu_a * _sigmoid(glu_b)
        _stage_conv_input(conv_in, jnp.where(first, 0.0, tail_scr[...]), xs_scr)
        tail_scr[...] = conv_in[TM - HALO:TM, :]

        scale = LOG2_E / (HEAD_DIM ** 0.5)
        qvt_ref[0:ATTN_WIDTH, :] = (_dot_nt(wqvt_ref[0:ATTN_WIDTH, :], u2_scr[...]) * scale).astype(BF16)
        qvt_ref[ATTN_WIDTH:2 * ATTN_WIDTH, :] = _dot_nt(wqvt_ref[ATTN_WIDTH:2 * ATTN_WIDTH, :],
                                                        u2_scr[...]).astype(BF16)
        k_ref[...] = _dot(u2_scr[...], wk_ref[...]).astype(BF16)

        lane1 = lax.broadcasted_iota(jnp.int32, (1, LANES), 1)
        f = _dot(u2_scr[...], wf_ref[...]) + bf_ref[...]
        ls = jnp.minimum(f, 0.0) - jnp.log1p(jnp.exp(-jnp.abs(f)))
        ls = jnp.where(lane1 < BIAS_LANES, ls, 0.0)
        dst = lax.broadcasted_iota(jnp.int32, (TM, TM), 0)
        src = lax.broadcasted_iota(jnp.int32, (TM, TM), 1)
        tri = jnp.where(src <= dst, 1.0, 0.0).astype(BF16)
        cs3 = _dot(tri, jnp.concatenate(_split3(ls), axis=1))
        carry = jnp.where(first, 0.0, carry_scr[0:1, :])
        c = cs3[:, 0:LANES] + cs3[:, LANES:2 * LANES] + cs3[:, 2 * LANES:3 * LANES] + carry
        carry_scr[...] = jnp.broadcast_to(c[TM - 1:TM, :], carry_scr.shape)
        hi, mid, lo = _split3(-LOG2_E * c)
        is_hi = functools.reduce(jnp.logical_or, [lane1 == PIECES * h for h in range(N_HEADS)])
        is_mid = functools.reduce(jnp.logical_or, [lane1 == PIECES * h + 1 for h in range(N_HEADS)])
        cb_ref[...] = jnp.where(is_hi, hi, jnp.where(is_mid, mid, lo))

    @pl.when(i == n_tiles)
    def _():
        _conv_module(xs_scr, ys_scr, cw_ref, cbias_ref, gc_ref, yc_ref)


def _ffn_proj_call(h, g_ff, w_in, w_out, g_mix, w_qvt, w_k, w_c, w_f, b_f, conv_w, conv_b, g_conv, layer):
    m = h.shape[0]
    n_tiles = m // TM
    per_seq = SEQ // TM
    cur = lambda i: jnp.minimum(i, n_tiles - 1)
    return pl.pallas_call(
        functools.partial(_ffn_proj_body, n_tiles=n_tiles),
        grid=(n_tiles + 1,),
        in_specs=[
            pl.BlockSpec((TM, D_MODEL), lambda i: (cur(i), 0)),
            _layer_block((1, D_MODEL), layer),
            _layer_block((D_MODEL, 2 * D_FF), layer),
            _layer_block((D_FF, D_MODEL), layer),
            _layer_block((1, D_MODEL), layer),
            _layer_block((2 * ATTN_WIDTH, D_MODEL), layer),
            _layer_block((D_MODEL, ATTN_WIDTH), layer),
            _layer_block((D_MODEL, 2 * CONV_CH), layer),
            _layer_block((D_MODEL, LANES), layer),
            _layer_block((1, LANES), layer),
            _layer_block((CONV_K, 8, CONV_CH), layer),
            _layer_block((1, CONV_CH), layer),
            _layer_block((1, CONV_CH), layer),
        ],
        out_specs=[
            pl.BlockSpec((TM, D_MODEL), lambda i: (cur(i), 0)),
            pl.BlockSpec((None, 2 * ATTN_WIDTH, TM), lambda i: (cur(i) // per_seq, 0, cur(i) % per_seq)),
            pl.BlockSpec((None, TM, ATTN_WIDTH), lambda i: (cur(i) // per_seq, cur(i) % per_seq, 0)),
            pl.BlockSpec((None, TM, LANES), lambda i: (cur(i) // per_seq, cur(i) % per_seq, 0)),
            pl.BlockSpec((TM, CONV_CH), lambda i: (jnp.maximum(i - 1, 0), 0)),
        ],
        out_shape=[
            jax.ShapeDtypeStruct((m, D_MODEL), F32),
            jax.ShapeDtypeStruct((BATCH, 2 * ATTN_WIDTH, SEQ), BF16),
            jax.ShapeDtypeStruct((BATCH, SEQ, ATTN_WIDTH), BF16),
            jax.ShapeDtypeStruct((BATCH, SEQ, LANES), BF16),
            jax.ShapeDtypeStruct((m, CONV_CH), BF16),
        ],
        scratch_shapes=[pltpu.VMEM((TM, D_MODEL), BF16),
                        pltpu.VMEM((TM, D_FF), BF16),
                        pltpu.VMEM((TM, D_MODEL), BF16),
                        pltpu.VMEM((CONV_CH // LANES, (TM // GROUP) * XS_PITCH, LANES), F32),
                        pltpu.VMEM((CONV_CH // LANES, (TM // GROUP) * YS_PITCH, LANES), F32),
                        pltpu.VMEM((HALO, CONV_CH), F32),
                        pltpu.VMEM((8, LANES), F32)],
        compiler_params=pltpu.CompilerParams(dimension_semantics=("arbitrary",),
                                             vmem_limit_bytes=VMEM_LIMIT_FFN_PROJ),
        name="ffn_proj",
    )(h, g_ff, w_in, w_out, g_mix, w_qvt, w_k, w_c, w_f, b_f, conv_w, conv_b, g_conv)


def _attn_body(qt_ref, k_ref, cb_ref, vt_ref, o_ref, qa_scr, s_scr, m_scr, acc_scr):
    i = pl.program_id(1)
    nq = 2 * TQ
    sub = lax.broadcasted_iota(jnp.int32, (LANES, TQ), 0)

    @pl.when((pl.program_id(0) == 0) & (i == 0))
    def _():
        srow = lax.broadcasted_iota(jnp.int32, (LANES, nq), 0)
        scol = lax.broadcasted_iota(jnp.int32, (LANES, nq), 1)
        for g in range(N_PAIRS):
            first = PIECES * jnp.where(scol < TQ, 2 * g, 2 * g + 1)
            sel = (srow >= first) & (srow < first + PIECES)
            qa_scr[g, LANES:2 * LANES, :] = jnp.where(sel, 1.0, 0.0).astype(BF16)

    for g in range(N_PAIRS):
        qt = qt_ref[g * LANES:(g + 1) * LANES, :]
        zero = jnp.zeros_like(qt)
        qa_scr[g, 0:LANES, :] = jnp.concatenate(
            [jnp.where(sub < HEAD_DIM, qt, zero), jnp.where(sub >= HEAD_DIM, qt, zero)], axis=1)
    m_scr[...] = jnp.full(m_scr.shape, MASKED, F32)
    acc_scr[...] = jnp.zeros_like(acc_scr)

    ones = jnp.ones((ONES_ROWS, TK), BF16)

    def scores(j, g):
        k0 = pl.multiple_of(j * TK, TK)
        ka = jnp.concatenate([k_ref[pl.ds(k0, TK), g * LANES:(g + 1) * LANES],
                              cb_ref[pl.ds(k0, TK), :]], axis=1)
        s_scr[g] = _dot(ka, qa_scr[g])

    def softmax_pv(j, g, diagonal):
        k0 = pl.multiple_of(j * TK, TK)
        st = s_scr[g]
        if diagonal:
            key = lax.broadcasted_iota(jnp.int32, (TK, nq), 0)
            qcol = lax.broadcasted_iota(jnp.int32, (TK, nq), 1)
            st = jnp.where(key <= jnp.where(qcol >= TQ, qcol - TQ, qcol), st, MASKED)
        m_old = m_scr[g]
        m_new = jnp.maximum(m_old, jnp.max(st, axis=0, keepdims=True))
        pt = jnp.exp2(st - m_new).astype(BF16)
        va = jnp.concatenate([vt_ref[g * LANES:(g + 1) * LANES, pl.ds(k0, TK)], ones], axis=0)
        acc_scr[g] = jnp.exp2(m_old - m_new) * acc_scr[g] + _dot(va, pt)
        m_scr[g] = m_new

    for g in range(N_PAIRS):
        scores(0, g)

    def body(j, carry):
        for g in range(N_PAIRS):
            softmax_pv(j, g, False)
            scores(j + 1, g)
        return carry

    lax.fori_loop(0, i, body, 0)
    for g in range(N_PAIRS):
        softmax_pv(i, g, True)

    for g in range(N_PAIRS):
        acc = acc_scr[g]
        l = acc[LANES:LANES + 1, :]
        out_t = jnp.concatenate([acc[0:HEAD_DIM, 0:TQ] / l[:, 0:TQ],
                                 acc[HEAD_DIM:LANES, TQ:nq] / l[:, TQ:nq]], axis=0)
        o_ref[:, g * LANES:(g + 1) * LANES] = out_t.T.astype(BF16)


def _attn_call(qv_t, k, c_b):
    assert TQ == TK
    whole = lambda b, i: (b, 0, 0)
    return pl.pallas_call(
        _attn_body,
        grid=(BATCH, SEQ // TQ),
        in_specs=[
            pl.BlockSpec((None, ATTN_WIDTH, TQ), lambda b, i: (b, 0, i)),
            pl.BlockSpec((None, SEQ, ATTN_WIDTH), whole),
            pl.BlockSpec((None, SEQ, LANES), whole),
            pl.BlockSpec((None, ATTN_WIDTH, SEQ), lambda b, i: (b, 1, 0)),
        ],
        out_specs=pl.BlockSpec((None, TQ, ATTN_WIDTH), lambda b, i: (b, i, 0)),
        out_shape=jax.ShapeDtypeStruct((BATCH, SEQ, ATTN_WIDTH), BF16),
        scratch_shapes=[pltpu.VMEM((N_PAIRS, 2 * LANES, 2 * TQ), BF16),
                        pltpu.VMEM((N_PAIRS, TK, 2 * TQ), F32),
                        pltpu.VMEM((N_PAIRS, 1, 2 * TQ), F32),
                        pltpu.VMEM((N_PAIRS, LANES + ONES_ROWS, 2 * TQ), F32)],
        compiler_params=pltpu.CompilerParams(dimension_semantics=("arbitrary", "arbitrary"),
                                             vmem_limit_bytes=VMEM_LIMIT),
        name="attn",
    )(qv_t, k, c_b, qv_t)


def _mix_body(h_ref, g_ref, oa_ref, yc_ref, wg_ref, wao_ref, wco_ref, wout_ref, o_ref, m_scr):
    x = h_ref[...]
    u = _rms(x, g_ref[...]).astype(BF16)
    oa = oa_ref[...]
    yc = yc_ref[...]
    for c in range(D_MODEL // MIX_CHUNK):
        lo = c * MIX_CHUNK
        ga = _sigmoid(_dot(u, wg_ref[:, lo:lo + MIX_CHUNK]))
        gc = _sigmoid(_dot(u, wg_ref[:, D_MODEL + lo:D_MODEL + lo + MIX_CHUNK]))
        ya = _dot(oa, wao_ref[:, lo:lo + MIX_CHUNK])
        yv = _dot(yc, wco_ref[:, lo:lo + MIX_CHUNK])
        m_scr[:, lo:lo + MIX_CHUNK] = (ga * ya + gc * yv).astype(BF16)
    o_ref[...] = x + _dot(m_scr[...], wout_ref[...])


def _mix_call(h, g, o_attn, y_conv, w_g, w_ao, w_co, w_out, layer):
    m = h.shape[0]
    row = lambda i: (i, 0)
    return pl.pallas_call(
        _mix_body,
        grid=(m // TM,),
        in_specs=[
            pl.BlockSpec((TM, D_MODEL), row),
            _layer_block((1, D_MODEL), layer),
            pl.BlockSpec((TM, ATTN_WIDTH), row),
            pl.BlockSpec((TM, CONV_CH), row),
            _layer_block((D_MODEL, 2 * D_MODEL), layer),
            _layer_block((ATTN_WIDTH, D_MODEL), layer),
            _layer_block((CONV_CH, D_MODEL), layer),
            _layer_block((D_MODEL, D_MODEL), layer),
        ],
        out_specs=pl.BlockSpec((TM, D_MODEL), row),
        out_shape=jax.ShapeDtypeStruct((m, D_MODEL), F32),
        scratch_shapes=[pltpu.VMEM((TM, D_MODEL), BF16)],
        compiler_params=pltpu.CompilerParams(dimension_semantics=("arbitrary",),
                                             vmem_limit_bytes=VMEM_LIMIT),
        name="mixout",
    )(h, g, o_attn, y_conv, w_g, w_ao, w_co, w_out)


def kernel(x, p, g_ff1, w_ff1_in, w_ff1_out, g_mix, w_in, b_f, w_attn_out, conv_w, conv_b, g_conv,
           w_conv_out, w_out, g_ff2, w_ff2_in, w_ff2_out, g_ple, w_ple_gate, w_ple_proj, g_final):
    assert x.shape == (BATCH, SEQ, D_MODEL) and p.shape == (DEPTH, BATCH, SEQ, D_PLE)
    m = BATCH * SEQ
    row3 = lambda a: a[:, None, :]

    w_qvt = jnp.swapaxes(jnp.concatenate([w_in[:, :, 0:ATTN_WIDTH], w_in[:, :, 2 * ATTN_WIDTH:F0]], axis=2),
                         1, 2).astype(BF16)
    w_k = w_in[:, :, ATTN_WIDTH:2 * ATTN_WIDTH].astype(BF16)
    rep = jnp.pad(jnp.repeat(w_in[:, :, F0:C0], PIECES, axis=2),
                  ((0, 0), (0, 0), (0, LANES - BIAS_LANES))).astype(BF16)
    b_rep = jnp.pad(jnp.repeat(b_f, PIECES, axis=1), ((0, 0), (0, LANES - BIAS_LANES)))[:, None, :]
    w_c = w_in[:, :, C0:GA0].astype(BF16)
    w_g = w_in[:, :, GA0:].astype(BF16)
    conv_wp = jnp.broadcast_to(conv_w[:, :, None, :], (DEPTH, CONV_K, 8, CONV_CH))
    p2 = p.reshape(DEPTH, m, D_PLE)
    g_ff1_r, g_mix_r, g_ff2_r, g_ple_r = row3(g_ff1), row3(g_mix), row3(g_ff2), row3(g_ple)
    conv_b_r, g_conv_r = row3(conv_b), row3(g_conv)
    g_final_r = g_final[None, :]

    h = x.reshape(m, D_MODEL)
    for i in range(DEPTH):
        h, qv_t, k, c_b, y_conv = _ffn_proj_call(h, g_ff1_r, w_ff1_in, w_ff1_out, g_mix_r,
                                                 w_qvt, w_k, w_c, rep, b_rep, conv_wp, conv_b_r, g_conv_r, i)
        o_attn = _attn_call(qv_t, k, c_b)
        h = _mix_call(h, g_mix_r, o_attn.reshape(m, ATTN_WIDTH), y_conv, w_g,
                      w_attn_out, w_conv_out, w_out, i)
        h = _ffn_ple_call(h, g_ff2_r, w_ff2_in, w_ff2_out, p2, g_ple_r, w_ple_gate, w_ple_proj, i,
                          g_final=g_final_r if i == DEPTH - 1 else None)
    return h.reshape(BATCH, SEQ, D_MODEL)
```

```python
import functools

import jax
import jax.numpy as jnp
from jax import lax
from jax.experimental import pallas as pl
from jax.experimental.pallas import tpu as pltpu

D_MODEL = 1024
BATCH = 8
SEQ = 2048
DEPTH = 4
N_HEADS = 8
HEAD_DIM = 64
ATTN_WIDTH = N_HEADS * HEAD_DIM
CONV_CH = 512
CONV_K = 31
D_FF = 2816
D_PLE = 256
EPS = 1e-6
FFN_RES = 0.5

F0 = 3 * ATTN_WIDTH
C0 = F0 + N_HEADS
GA0 = C0 + 2 * CONV_CH

LANES = 128
N_PAIRS = N_HEADS // 2
PIECES = 3
BIAS_LANES = PIECES * N_HEADS
ONES_ROWS = 16
HALO = 32
TM = 512
FFN_CHUNK = 256
MIX_CHUNK = 256
TQ = 512
TK = 512
GROUP = TM // 8
XS_PITCH = HALO + GROUP + 4
YS_PITCH = GROUP + 4
CONV_ROWS = 16
MASKED = -1e30
LOG2_E = 1.4426950408889634
VMEM_LIMIT = 56 * 1024 * 1024
VMEM_LIMIT_FFN_PROJ = 60 * 1024 * 1024

BF16 = jnp.bfloat16
F32 = jnp.float32


def _dot(a, b):
    return jnp.dot(a, b.astype(BF16), preferred_element_type=F32)


def _dot_nt(a, b):
    return lax.dot_general(a, b, (((1,), (1,)), ((), ())), preferred_element_type=F32)


def _rms(x, g):
    ms = jnp.mean(x * x, axis=-1, keepdims=True)
    return x * lax.rsqrt(ms + EPS) * g


def _sigmoid(x):
    return 1.0 / (1.0 + jnp.exp(-x))


def _split3(x):
    hi = x.astype(BF16)
    r1 = x - hi.astype(F32)
    mid = r1.astype(BF16)
    lo = (r1 - mid.astype(F32)).astype(BF16)
    return hi, mid, lo


def _resident(shape):
    nd = len(shape)
    return pl.BlockSpec(shape, lambda *_: (0,) * nd, pipeline_mode=pl.Buffered(1))


def _layer_block(shape, layer):
    nd = len(shape)
    return pl.BlockSpec((None,) + tuple(shape), lambda *_: (layer,) + (0,) * nd,
                        pipeline_mode=pl.Buffered(1))


def _swiglu_half_step(x, g_ref, win_ref, wout_ref, u_scr, act_scr):
    u_scr[...] = _rms(x, g_ref[...]).astype(BF16)
    for c in range(D_FF // FFN_CHUNK):
        lo = c * FFN_CHUNK
        u = u_scr[...]
        a = _dot(u, win_ref[:, lo:lo + FFN_CHUNK])
        b = _dot(u, win_ref[:, D_FF + lo:D_FF + lo + FFN_CHUNK])
        act_scr[:, lo:lo + FFN_CHUNK] = (a * _sigmoid(a) * b).astype(BF16)
    return x + FFN_RES * _dot(act_scr[...], wout_ref[...])


def _ffn_ple_body(*refs, with_final):
    h_ref, g_ref, win_ref, wout_ref, p_ref, gple_ref, wpg_ref, wpp_ref = refs[:8]
    pos = 8
    if with_final:
        gfin_ref = refs[pos]
        pos += 1
    o_ref, u_scr, act_scr = refs[pos:pos + 3]

    h1 = _swiglu_half_step(h_ref[...], g_ref, win_ref, wout_ref, u_scr, act_scr)
    u2 = _rms(h1, gple_ref[...]).astype(BF16)
    gate = _sigmoid(_dot(u2, wpg_ref[...]))
    h1 = h1 + gate * _dot(p_ref[...].astype(BF16), wpp_ref[...])
    if with_final:
        h1 = _rms(h1, gfin_ref[...])
    o_ref[...] = h1


def _ffn_ple_call(h, g, w_in, w_out, p, g_ple, w_pg, w_pp, layer, g_final=None):
    m = h.shape[0]
    row = lambda i: (i, 0)
    in_specs = [
        pl.BlockSpec((TM, D_MODEL), row),
        _layer_block((1, D_MODEL), layer),
        _layer_block((D_MODEL, 2 * D_FF), layer),
        _layer_block((D_FF, D_MODEL), layer),
        pl.BlockSpec((None, TM, D_PLE), lambda i: (layer, i, 0)),
        _layer_block((1, D_MODEL), layer),
        _layer_block((D_MODEL, D_MODEL), layer),
        _layer_block((D_PLE, D_MODEL), layer),
    ]
    args = [h, g, w_in, w_out, p, g_ple, w_pg, w_pp]
    if g_final is not None:
        in_specs.append(_resident((1, D_MODEL)))
        args.append(g_final)
    return pl.pallas_call(
        functools.partial(_ffn_ple_body, with_final=g_final is not None),
        grid=(m // TM,),
        in_specs=in_specs,
        out_specs=pl.BlockSpec((TM, D_MODEL), row),
        out_shape=jax.ShapeDtypeStruct((m, D_MODEL), F32),
        scratch_shapes=[pltpu.VMEM((TM, D_MODEL), BF16), pltpu.VMEM((TM, D_FF), BF16)],
        compiler_params=pltpu.CompilerParams(dimension_semantics=("arbitrary",),
                                             vmem_limit_bytes=VMEM_LIMIT),
        name="ffn_ple",
    )(*args)


def _stage_conv_input(av, tail_prev, xs_scr):
    for s in range(CONV_CH // LANES):
        cols = slice(s * LANES, (s + 1) * LANES)
        for b in range(TM // GROUP):
            base = b * XS_PITCH
            hist = tail_prev[:, cols] if b == 0 else av[b * GROUP - HALO:b * GROUP, cols]
            xs_scr[s, base:base + HALO, :] = hist
            xs_scr[s, base + HALO:base + HALO + GROUP, :] = av[b * GROUP:(b + 1) * GROUP, cols]


def _conv_module(xs_scr, ys_scr, cw_ref, cb_ref, gc_ref, yc_ref):
    n_groups = TM // GROUP
    for s in range(CONV_CH // LANES):
        cols = slice(s * LANES, (s + 1) * LANES)
        bias = jnp.broadcast_to(cb_ref[:, cols], (n_groups, LANES))
        for a0 in range(0, GROUP, CONV_ROWS):
            accs = [bias] * CONV_ROWS
            for d in range(CONV_K):
                w = cw_ref[CONV_K - 1 - d, :, cols]
                for j in range(CONV_ROWS):
                    x = xs_scr[s, pl.ds(HALO + a0 + j - d, n_groups, stride=XS_PITCH), :]
                    accs[j] = accs[j] + w * x
            for j in range(CONV_ROWS):
                ys_scr[s, pl.ds(a0 + j, n_groups, stride=YS_PITCH), :] = accs[j]
    for b in range(n_groups):
        y = jnp.concatenate([ys_scr[s, b * YS_PITCH:b * YS_PITCH + GROUP, :]
                             for s in range(CONV_CH // LANES)], axis=1)
        y = _rms(y, gc_ref[...])
        yc_ref[b * GROUP:(b + 1) * GROUP, :] = (y * _sigmoid(y)).astype(BF16)


def _ffn_proj_body(h_ref, gff_ref, win_ref, wout_ref, gmix_ref, wqvt_ref, wk_ref, wc_ref, wf_ref, bf_ref,
                   cw_ref, cbias_ref, gc_ref,
                   h1_ref, qvt_ref, k_ref, cb_ref, yc_ref,
                   u_scr, act_scr, u2_scr, xs_scr, ys_scr, tail_scr, carry_scr, *, n_tiles):
    i = pl.program_id(0)

    @pl.when(i == 0)
    def _():
        xs_scr[...] = jnp.zeros_like(xs_scr)
        ys_scr[...] = jnp.zeros_like(ys_scr)
        tail_scr[...] = jnp.zeros_like(tail_scr)
        carry_scr[...] = jnp.zeros_like(carry_scr)

    @pl.when(i < n_tiles)
    def _():
        _conv_module(xs_scr, ys_scr, cw_ref, cbias_ref, gc_ref, yc_ref)

        h1 = _swiglu_half_step(h_ref[...], gff_ref, win_ref, wout_ref, u_scr, act_scr)
        h1_ref[...] = h1

        first = (i % (SEQ // TM)) == 0
        u2_scr[...] = _rms(h1, gmix_ref[...]).astype(BF16)
        glu_a = _dot(u2_scr[...], wc_ref[:, 0:CONV_CH])
        glu_b = _dot(u2_scr[...], wc_ref[:, CONV_CH:2 * CONV_CH])
        conv_in = glu_a * _sigmoid(glu_b)
        _stage_conv_input(conv_in, jnp.where(first, 0.0, tail_scr[...]), xs_scr)
        tail_scr[...] = conv_in[TM - HALO:TM, :]

        scale = LOG2_E / (HEAD_DIM ** 0.5)
        qvt_ref[0:ATTN_WIDTH, :] = (_dot_nt(wqvt_ref[0:ATTN_WIDTH, :], u2_scr[...]) * scale).astype(BF16)
        qvt_ref[ATTN_WIDTH:2 * ATTN_WIDTH, :] = _dot_nt(wqvt_ref[ATTN_WIDTH:2 * ATTN_WIDTH, :],
                                                        u2_scr[...]).astype(BF16)
        k_ref[...] = _dot(u2_scr[...], wk_ref[...]).astype(BF16)

        lane1 = lax.broadcasted_iota(jnp.int32, (1, LANES), 1)
        f = _dot(u2_scr[...], wf_ref[...]) + bf_ref[...]
        ls = jnp.minimum(f, 0.0) - jnp.log1p(jnp.exp(-jnp.abs(f)))
        ls = jnp.where(lane1 < BIAS_LANES, ls, 0.0)
        dst = lax.broadcasted_iota(jnp.int32, (TM, TM), 0)
        src = lax.broadcasted_iota(jnp.int32, (TM, TM), 1)
        tri = jnp.where(src <= dst, 1.0, 0.0).astype(BF16)
        cs3 = _dot(tri, jnp.concatenate(_split3(ls), axis=1))
        carry = jnp.where(first, 0.0, carry_scr[0:1, :])
        c = cs3[:, 0:LANES] + cs3[:, LANES:2 * LANES] + cs3[:, 2 * LANES:3 * LANES] + carry
        carry_scr[...] = jnp.broadcast_to(c[TM - 1:TM, :], carry_scr.shape)
        hi, mid, lo = _split3(-LOG2_E * c)
        is_hi = functools.reduce(jnp.logical_or, [lane1 == PIECES * h for h in range(N_HEADS)])
        is_mid = functools.reduce(jnp.logical_or, [lane1 == PIECES * h + 1 for h in range(N_HEADS)])
        cb_ref[...] = jnp.where(is_hi, hi, jnp.where(is_mid, mid, lo))

    @pl.when(i == n_tiles)
    def _():
        _conv_module(xs_scr, ys_scr, cw_ref, cbias_ref, gc_ref, yc_ref)


def _ffn_proj_call(h, g_ff, w_in, w_out, g_mix, w_qvt, w_k, w_c, w_f, b_f, conv_w, conv_b, g_conv, layer):
    m = h.shape[0]
    n_tiles = m // TM
    per_seq = SEQ // TM
    cur = lambda i: jnp.minimum(i, n_tiles - 1)
    return pl.pallas_call(
        functools.partial(_ffn_proj_body, n_tiles=n_tiles),
        grid=(n_tiles + 1,),
        in_specs=[
            pl.BlockSpec((TM, D_MODEL), lambda i: (cur(i), 0)),
            _layer_block((1, D_MODEL), layer),
            _layer_block((D_MODEL, 2 * D_FF), layer),
            _layer_block((D_FF, D_MODEL), layer),
            _layer_block((1, D_MODEL), layer),
            _layer_block((2 * ATTN_WIDTH, D_MODEL), layer),
            _layer_block((D_MODEL, ATTN_WIDTH), layer),
            _layer_block((D_MODEL, 2 * CONV_CH), layer),
            _layer_block((D_MODEL, LANES), layer),
            _layer_block((1, LANES), layer),
            _layer_block((CONV_K, 8, CONV_CH), layer),
            _layer_block((1, CONV_CH), layer),
            _layer_block((1, CONV_CH), layer),
        ],
        out_specs=[
            pl.BlockSpec((TM, D_MODEL), lambda i: (cur(i), 0)),
            pl.BlockSpec((None, 2 * ATTN_WIDTH, TM), lambda i: (cur(i) // per_seq, 0, cur(i) % per_seq)),
            pl.BlockSpec((None, TM, ATTN_WIDTH), lambda i: (cur(i) // per_seq, cur(i) % per_seq, 0)),
            pl.BlockSpec((None, TM, LANES), lambda i: (cur(i) // per_seq, cur(i) % per_seq, 0)),
            pl.BlockSpec((TM, CONV_CH), lambda i: (jnp.maximum(i - 1, 0), 0)),
        ],
        out_shape=[
            jax.ShapeDtypeStruct((m, D_MODEL), F32),
            jax.ShapeDtypeStruct((BATCH, 2 * ATTN_WIDTH, SEQ), BF16),
            jax.ShapeDtypeStruct((BATCH, SEQ, ATTN_WIDTH), BF16),
            jax.ShapeDtypeStruct((BATCH, SEQ, LANES), BF16),
            jax.ShapeDtypeStruct((m, CONV_CH), BF16),
        ],
        scratch_shapes=[pltpu.VMEM((TM, D_MODEL), BF16),
                        pltpu.VMEM((TM, D_FF), BF16),
                        pltpu.VMEM((TM, D_MODEL), BF16),
                        pltpu.VMEM((CONV_CH // LANES, (TM // GROUP) * XS_PITCH, LANES), F32),
                        pltpu.VMEM((CONV_CH // LANES, (TM // GROUP) * YS_PITCH, LANES), F32),
                        pltpu.VMEM((HALO, CONV_CH), F32),
                        pltpu.VMEM((8, LANES), F32)],
        compiler_params=pltpu.CompilerParams(dimension_semantics=("arbitrary",),
                                             vmem_limit_bytes=VMEM_LIMIT_FFN_PROJ),
        name="ffn_proj",
    )(h, g_ff, w_in, w_out, g_mix, w_qvt, w_k, w_c, w_f, b_f, conv_w, conv_b, g_conv)


def _attn_body(qt_ref, k_ref, cb_ref, vt_ref, o_ref, qa_scr, s_scr, m_scr, acc_scr):
    i = pl.program_id(1)
    nq = 2 * TQ
    sub = lax.broadcasted_iota(jnp.int32, (LANES, TQ), 0)

    @pl.when((pl.program_id(0) == 0) & (i == 0))
    def _():
        srow = lax.broadcasted_iota(jnp.int32, (LANES, nq), 0)
        scol = lax.broadcasted_iota(jnp.int32, (LANES, nq), 1)
        for g in range(N_PAIRS):
            first = PIECES * jnp.where(scol < TQ, 2 * g, 2 * g + 1)
            sel = (srow >= first) & (srow < first + PIECES)
            qa_scr[g, LANES:2 * LANES, :] = jnp.where(sel, 1.0, 0.0).astype(BF16)

    for g in range(N_PAIRS):
        qt = qt_ref[g * LANES:(g + 1) * LANES, :]
        zero = jnp.zeros_like(qt)
        qa_scr[g, 0:LANES, :] = jnp.concatenate(
            [jnp.where(sub < HEAD_DIM, qt, zero), jnp.where(sub >= HEAD_DIM, qt, zero)], axis=1)
    m_scr[...] = jnp.full(m_scr.shape, MASKED, F32)
    acc_scr[...] = jnp.zeros_like(acc_scr)

    ones = jnp.ones((ONES_ROWS, TK), BF16)

    def scores(j, g):
        k0 = pl.multiple_of(j * TK, TK)
        ka = jnp.concatenate([k_ref[pl.ds(k0, TK), g * LANES:(g + 1) * LANES],
                              cb_ref[pl.ds(k0, TK), :]], axis=1)
        s_scr[g] = _dot(ka, qa_scr[g])

    def softmax_pv(j, g, diagonal):
        k0 = pl.multiple_of(j * TK, TK)
        st = s_scr[g]
        if diagonal:
            key = lax.broadcasted_iota(jnp.int32, (TK, nq), 0)
            qcol = lax.broadcasted_iota(jnp.int32, (TK, nq), 1)
            st = jnp.where(key <= jnp.where(qcol >= TQ, qcol - TQ, qcol), st, MASKED)
        m_old = m_scr[g]
        m_new = jnp.maximum(m_old, jnp.max(st, axis=0, keepdims=True))
        pt = jnp.exp2(st - m_new).astype(BF16)
        va = jnp.concatenate([vt_ref[g * LANES:(g + 1) * LANES, pl.ds(k0, TK)], ones], axis=0)
        acc_scr[g] = jnp.exp2(m_old - m_new) * acc_scr[g] + _dot(va, pt)
        m_scr[g] = m_new

    for g in range(N_PAIRS):
        scores(0, g)

    def body(j, carry):
        for g in range(N_PAIRS):
            softmax_pv(j, g, False)
            scores(j + 1, g)
        return carry

    def softmax_pv_diagonal(g):
        half = TK // 2
        rows = slice(g * LANES, (g + 1) * LANES)
        ones_h = jnp.ones((ONES_ROWS, half), BF16)

        def update(keys, lanes, visible):
            k0 = pl.multiple_of(i * TK + keys.start, half)
            st = jnp.where(visible, s_scr[g, keys, lanes], MASKED)
            m_old = m_scr[g, :, lanes]
            m_new = jnp.maximum(m_old, jnp.max(st, axis=0, keepdims=True))
            pt = jnp.exp2(st - m_new).astype(BF16)
            va = jnp.concatenate([vt_ref[rows, pl.ds(k0, half)], ones_h], axis=0)
            acc_scr[g, :, lanes] = jnp.exp2(m_old - m_new) * acc_scr[g, :, lanes] + _dot(va, pt)
            m_scr[g, :, lanes] = m_new

        key = lax.broadcasted_iota(jnp.int32, (half, nq), 0)
        qcol = lax.broadcasted_iota(jnp.int32, (half, nq), 1)
        update(slice(0, half), slice(0, nq), key <= jnp.where(qcol >= TQ, qcol - TQ, qcol))
        key_h = lax.broadcasted_iota(jnp.int32, (half, half), 0)
        q_h = lax.broadcasted_iota(jnp.int32, (half, half), 1)
        for lanes in (slice(half, TQ), slice(TQ + half, nq)):
            update(slice(half, TK), lanes, key_h <= q_h)

    lax.fori_loop(0, i, body, 0)
    for g in range(N_PAIRS):
        softmax_pv_diagonal(g)

    for g in range(N_PAIRS):
        acc = acc_scr[g]
        l = acc[LANES:LANES + 1, :]
        out_t = jnp.concatenate([acc[0:HEAD_DIM, 0:TQ] / l[:, 0:TQ],
                                 acc[HEAD_DIM:LANES, TQ:nq] / l[:, TQ:nq]], axis=0)
        o_ref[:, g * LANES:(g + 1) * LANES] = out_t.T.astype(BF16)


def _attn_call(qv_t, k, c_b):
    assert TQ == TK
    whole = lambda b, i: (b, 0, 0)
    return pl.pallas_call(
        _attn_body,
        grid=(BATCH, SEQ // TQ),
        in_specs=[
            pl.BlockSpec((None, ATTN_WIDTH, TQ), lambda b, i: (b, 0, i)),
            pl.BlockSpec((None, SEQ, ATTN_WIDTH), whole),
            pl.BlockSpec((None, SEQ, LANES), whole),
            pl.BlockSpec((None, ATTN_WIDTH, SEQ), lambda b, i: (b, 1, 0)),
        ],
        out_specs=pl.BlockSpec((None, TQ, ATTN_WIDTH), lambda b, i: (b, i, 0)),
        out_shape=jax.ShapeDtypeStruct((BATCH, SEQ, ATTN_WIDTH), BF16),
        scratch_shapes=[pltpu.VMEM((N_PAIRS, 2 * LANES, 2 * TQ), BF16),
                        pltpu.VMEM((N_PAIRS, TK, 2 * TQ), F32),
                        pltpu.VMEM((N_PAIRS, 1, 2 * TQ), F32),
                        pltpu.VMEM((N_PAIRS, LANES + ONES_ROWS, 2 * TQ), F32)],
        compiler_params=pltpu.CompilerParams(dimension_semantics=("arbitrary", "arbitrary"),
                                             vmem_limit_bytes=VMEM_LIMIT),
        name="attn",
    )(qv_t, k, c_b, qv_t)


def _mix_body(h_ref, g_ref, oa_ref, yc_ref, wg_ref, wao_ref, wco_ref, wout_ref, o_ref, m_scr):
    x = h_ref[...]
    u = _rms(x, g_ref[...]).astype(BF16)
    oa = oa_ref[...]
    yc = yc_ref[...]
    for c in range(D_MODEL // MIX_CHUNK):
        lo = c * MIX_CHUNK
        ga = _sigmoid(_dot(u, wg_ref[:, lo:lo + MIX_CHUNK]))
        gc = _sigmoid(_dot(u, wg_ref[:, D_MODEL + lo:D_MODEL + lo + MIX_CHUNK]))
        ya = _dot(oa, wao_ref[:, lo:lo + MIX_CHUNK])
        yv = _dot(yc, wco_ref[:, lo:lo + MIX_CHUNK])
        m_scr[:, lo:lo + MIX_CHUNK] = (ga * ya + gc * yv).astype(BF16)
    o_ref[...] = x + _dot(m_scr[...], wout_ref[...])


def _mix_call(h, g, o_attn, y_conv, w_g, w_ao, w_co, w_out, layer):
    m = h.shape[0]
    row = lambda i: (i, 0)
    return pl.pallas_call(
        _mix_body,
        grid=(m // TM,),
        in_specs=[
            pl.BlockSpec((TM, D_MODEL), row),
            _layer_block((1, D_MODEL), layer),
            pl.BlockSpec((TM, ATTN_WIDTH), row),
            pl.BlockSpec((TM, CONV_CH), row),
            _layer_block((D_MODEL, 2 * D_MODEL), layer),
            _layer_block((ATTN_WIDTH, D_MODEL), layer),
            _layer_block((CONV_CH, D_MODEL), layer),
            _layer_block((D_MODEL, D_MODEL), layer),
        ],
        out_specs=pl.BlockSpec((TM, D_MODEL), row),
        out_shape=jax.ShapeDtypeStruct((m, D_MODEL), F32),
        scratch_shapes=[pltpu.VMEM((TM, D_MODEL), BF16)],
        compiler_params=pltpu.CompilerParams(dimension_semantics=("arbitrary",),
                                             vmem_limit_bytes=VMEM_LIMIT),
        name="mixout",
    )(h, g, o_attn, y_conv, w_g, w_ao, w_co, w_out)


def kernel(x, p, g_ff1, w_ff1_in, w_ff1_out, g_mix, w_in, b_f, w_attn_out, conv_w, conv_b, g_conv,
           w_conv_out, w_out, g_ff2, w_ff2_in, w_ff2_out, g_ple, w_ple_gate, w_ple_proj, g_final):
    assert x.shape == (BATCH, SEQ, D_MODEL) and p.shape == (DEPTH, BATCH, SEQ, D_PLE)
    m = BATCH * SEQ
    row3 = lambda a: a[:, None, :]

    w_qvt = jnp.swapaxes(jnp.concatenate([w_in[:, :, 0:ATTN_WIDTH], w_in[:, :, 2 * ATTN_WIDTH:F0]], axis=2),
                         1, 2).astype(BF16)
    w_k = w_in[:, :, ATTN_WIDTH:2 * ATTN_WIDTH].astype(BF16)
    rep = jnp.pad(jnp.repeat(w_in[:, :, F0:C0], PIECES, axis=2),
                  ((0, 0), (0, 0), (0, LANES - BIAS_LANES))).astype(BF16)
    b_rep = jnp.pad(jnp.repeat(b_f, PIECES, axis=1), ((0, 0), (0, LANES - BIAS_LANES)))[:, None, :]
    w_c = w_in[:, :, C0:GA0].astype(BF16)
    w_g = w_in[:, :, GA0:].astype(BF16)
    conv_wp = jnp.broadcast_to(conv_w[:, :, None, :], (DEPTH, CONV_K, 8, CONV_CH))
    p2 = p.reshape(DEPTH, m, D_PLE)
    g_ff1_r, g_mix_r, g_ff2_r, g_ple_r = row3(g_ff1), row3(g_mix), row3(g_ff2), row3(g_ple)
    conv_b_r, g_conv_r = row3(conv_b), row3(g_conv)
    g_final_r = g_final[None, :]

    h = x.reshape(m, D_MODEL)
    for i in range(DEPTH):
        h, qv_t, k, c_b, y_conv = _ffn_proj_call(h, g_ff1_r, w_ff1_in, w_ff1_out, g_mix_r,
                                                 w_qvt, w_k, w_c, rep, b_rep, conv_wp, conv_b_r, g_conv_r, i)
        o_attn = _attn_call(qv_t, k, c_b)
        h = _mix_call(h, g_mix_r, o_attn.reshape(m, ATTN_WIDTH), y_conv, w_g,
                      w_attn_out, w_conv_out, w_out, i)
        h = _ffn_ple_call(h, g_ff2_r, w_ff2_in, w_ff2_out, p2, g_ple_r, w_ple_gate, w_ple_proj, i,
                          g_final=g_final_r if i == DEPTH - 1 else None)
    return h.reshape(BATCH, SEQ, D_MODEL)
```
